```python
import math
import jax, jax.numpy as jnp
from jax import lax
import numpy as np

D_MODEL = 4096
BATCH = 4
SEQ = 2048
DEPTH = 2
DEC_BATCH = 8
DEC_SEQ = 8
PAST_LEN = 16384
PAGE_SIZE = 128

HEAD_DIM = 128
BR_WIDTH = D_MODEL // 4
H_A = BR_WIDTH // HEAD_DIM
H_B = BR_WIDTH // HEAD_DIM
H_C = BR_WIDTH // HEAD_DIM
N_BRANCH = 3
CONV_W = 4
CHUNK = 64
SB_BLOCK = 128
SB_BIAS_INIT = -9.0
N_GROUPS = 4
EXPERTS_PER_GROUP = 8
N_EXPERTS = N_GROUPS * EXPERTS_PER_GROUP
TOP_K_INNER = 2
D_EXPERT = D_MODEL // 8
EPS = 1e-6
SPLIT_SIZES = (BR_WIDTH,) * 4 + (BR_WIDTH,) * 4 + (H_B, H_B) + (BR_WIDTH,) * 3 + (N_BRANCH * D_MODEL,)
IN_COLS = sum(SPLIT_SIZES)

kernel_name = 'hybrid_hgrn2_gdn_stickbreak_hmoe_step'


def _rmsnorm(x, w):
    x32 = x.astype(jnp.float32)
    y = x32 * lax.rsqrt(jnp.mean(x32 * x32, axis=-1, keepdims=True) + EPS)
    return (y * w.astype(jnp.float32)).astype(x.dtype)


def _heads(a):
    return a.reshape(a.shape[:-1] + (a.shape[-1] // HEAD_DIM, HEAD_DIM))


def _l2norm(a):
    return a * lax.rsqrt(jnp.sum(a * a, axis=-1, keepdims=True) + EPS)


def _gated_head_norm(o, w, z):
    o = o * lax.rsqrt(jnp.mean(o * o, axis=-1, keepdims=True) + EPS) * w.astype(jnp.float32)
    o = o * jax.nn.silu(_heads(z))
    return o.reshape(o.shape[:2] + (-1,))


def _chunk_len(t, c):
    return c if t % c == 0 else t


def _to_chunks(a, c):
    b, t, h = a.shape[:3]
    a = a.reshape((b, t // c, c, h) + a.shape[3:])
    return a.transpose((1, 0, 3, 2) + tuple(range(4, a.ndim)))


def _from_chunks(o):
    n, b, h, c, d = o.shape
    return o.transpose(1, 0, 3, 2, 4).reshape(b, n * c, h, d)


def _hgrn2_chunked(q, k, v, log_f, s0):
    c = _chunk_len(q.shape[1], CHUNK)
    causal = jnp.tril(jnp.ones((c, c), dtype=bool))[:, :, None]

    def step(s, inp):
        qi, ki, vi, gi = inp
        b = jnp.cumsum(gi, axis=2)
        diff = b[:, :, :, None, :] - b[:, :, None, :, :]
        decay = jnp.exp(jnp.where(causal, diff, -jnp.inf))
        scores = jnp.einsum('bhtd,bhsd,bhtsd->bhts', qi, ki, decay)
        o = jnp.einsum('bhts,bhsv->bhtv', scores, vi) + jnp.einsum('bhtd,bhdv->bhtv', qi * jnp.exp(b), s)
        b_end = b[:, :, -1:, :]
        s_new = jnp.exp(b_end[:, :, 0, :])[..., None] * s + jnp.einsum('bhsd,bhsv->bhdv', ki * jnp.exp(b_end - b), vi)
        return s_new, o

    s_fin, o = lax.scan(step, s0, tuple(_to_chunks(a, c) for a in (q, k, v, log_f)))
    return _from_chunks(o), s_fin


def _gdn_chunked(q, k, v, g, beta, s0):
    c = _chunk_len(q.shape[1], CHUNK)
    incl = jnp.tril(jnp.ones((c, c), dtype=bool))
    strict = jnp.tril(jnp.ones((c, c), dtype=bool), -1)
    eye = jnp.eye(c, dtype=jnp.float32)

    def step(s, inp):
        qi, ki, vi, gi, bi = inp
        cg = jnp.cumsum(gi, axis=-1)
        gam = jnp.exp(jnp.where(incl, cg[..., :, None] - cg[..., None, :], -jnp.inf))
        kk = jnp.einsum('bhtd,bhsd->bhts', ki, ki)
        lower = jnp.where(strict, bi[..., :, None] * gam * kk, 0.0)
        rhs = bi[..., None] * (vi - jnp.exp(cg)[..., None] * jnp.einsum('bhtd,bhdv->bhtv', ki, s))
        u = lax.linalg.triangular_solve(eye + lower, rhs, left_side=True, lower=True, unit_diagonal=True)
        qk = jnp.einsum('bhtd,bhsd->bhts', qi, ki) * gam
        o = jnp.exp(cg)[..., None] * jnp.einsum('bhtd,bhdv->bhtv', qi, s) + jnp.einsum('bhts,bhsv->bhtv', qk, u)
        w_end = jnp.exp(cg[..., -1:] - cg)[..., None]
        s_new = jnp.exp(cg[..., -1])[..., None, None] * s + jnp.einsum('bhsd,bhsv->bhdv', ki * w_end, u)
        return s_new, o

    s_fin, o = lax.scan(step, s0, tuple(_to_chunks(a, c) for a in (q, k, v, g, beta)))
    return _from_chunks(o), s_fin


def _stick_breaking(q, k, v, bias, q_start):
    b, tq, h, hd = q.shape
    tk = k.shape[1]
    blk = _chunk_len(tq, SB_BLOCK)
    nb = tq // blk
    qb = q.reshape(b, nb, blk, h, hd).transpose(1, 0, 3, 2, 4)
    key_pos = jnp.arange(tk)
    scale = HEAD_DIM ** -0.5
    bias = bias.astype(jnp.float32)[None, :, None, None]

    def one_block(args):
        qi, i = args
        q_pos = q_start + i * blk + jnp.arange(blk)
        mask = key_pos[None, :] < q_pos[:, None]
        z = jnp.einsum('bhtd,bshd->bhts', qi, k) * scale + bias
        log_stay = jnp.where(mask, jax.nn.log_sigmoid(-z), 0.0)
        log_later = lax.cumsum(log_stay, axis=3, reverse=True) - log_stay
        w = jnp.where(mask, jnp.exp(jax.nn.log_sigmoid(z) + log_later), 0.0)
        return jnp.einsum('bhts,bshd->bhtd', w, v)

    o = lax.map(one_block, (qb, jnp.arange(nb)))
    return o.transpose(1, 0, 3, 2, 4).reshape(b, tq, h, hd)


def _causal_conv(x, buf, w):
    t = x.shape[1]
    xp = jnp.concatenate([buf, x], axis=1)
    y = xp[:, 0:t] * w[0]
    for j in range(1, CONV_W):
        y = y + xp[:, j:j + t] * w[j]
    return jax.nn.silu(y), xp[:, xp.shape[1] - (CONV_W - 1):]


def _hier_moe(x, w_rg, b_rg, w_re, b_re, w_g, w_u, w_d):
    n = x.shape[0]
    lg = (x @ w_rg).astype(jnp.float32) + b_rg.astype(jnp.float32)
    grp = jnp.argmax(lg, axis=-1)
    p_grp = jnp.take_along_axis(jax.nn.softmax(lg, axis=-1), grp[:, None], axis=-1)
    le = ((x @ w_re).astype(jnp.float32) + b_re.astype(jnp.float32)).reshape(n, N_GROUPS, EXPERTS_PER_GROUP)
    le = jnp.take_along_axis(le, grp[:, None, None], axis=1)[:, 0]
    top_v, top_i = lax.top_k(le, TOP_K_INNER)
    w_sel = jax.nn.softmax(top_v, axis=-1) * p_grp
    eid = grp[:, None] * EXPERTS_PER_GROUP + top_i
    combine = jnp.sum(jax.nn.one_hot(eid, N_EXPERTS, dtype=jnp.float32) * w_sel[..., None], axis=1)
    hid = jax.nn.silu(jnp.einsum('nd,edf->nef', x, w_g)) * jnp.einsum('nd,edf->nef', x, w_u)
    hid = hid * combine[..., None].astype(hid.dtype)
    return jnp.einsum('nef,efd->nd', hid, w_d)


def _decoder_layer(x, l, s_hgrn, s_gdn, conv_buf, past_k, past_v,
                   norm_mix, w_in, hgrn_lb, hgrn_norm, gdn_conv, gdn_a_log, gdn_dt_bias, gdn_norm, sb_bias,
                   w_branch, w_out, norm_ffn, w_router_group, b_router_group, w_router_expert,
                   b_router_expert, w_exp_gate, w_exp_up, w_exp_down):
    f32 = jnp.float32
    bsz, t, d = x.shape
    h = _rmsnorm(x, norm_mix[l])
    proj = (h @ w_in[l]).astype(f32)
    points = np.cumsum(SPLIT_SIZES)[:-1].tolist()
    (a_q, a_f, a_i, a_g, b_q, b_k, b_v, b_z, b_a, b_b, c_q, c_k, c_v, g_mix) = jnp.split(proj, points, axis=-1)

    lb_all = jnp.cumsum(jax.nn.softmax(hgrn_lb.astype(f32), axis=0), axis=0)
    lb = lb_all[l] - lb_all[0]
    log_f = jnp.logaddexp(jnp.log(lb), jnp.log1p(-lb) + jax.nn.log_sigmoid(a_f))
    k_a = (1.0 - lb) * jax.nn.sigmoid(-a_f)
    o_a, s_hgrn_new = _hgrn2_chunked(_heads(jax.nn.silu(a_q)), _heads(k_a), _heads(a_i), _heads(log_f),
                                     s_hgrn.astype(f32))
    o_a = _gated_head_norm(o_a, hgrn_norm[l], a_g)

    qkv, conv_new = _causal_conv(jnp.concatenate([b_q, b_k, b_v], axis=-1), conv_buf.astype(f32),
                                 gdn_conv[l].astype(f32))
    q_b, k_b, v_b = jnp.split(qkv, 3, axis=-1)
    q_b = _l2norm(_heads(q_b)) * HEAD_DIM ** -0.5
    k_b = _l2norm(_heads(k_b))
    g_b = -jnp.exp(gdn_a_log[l].astype(f32)) * jax.nn.softplus(b_a + gdn_dt_bias[l].astype(f32))
    beta = jax.nn.sigmoid(b_b)
    o_b, s_gdn_new = _gdn_chunked(q_b, k_b, _heads(v_b), g_b, beta, s_gdn.astype(f32))
    o_b = _gated_head_norm(o_b, gdn_norm[l], b_z)

    k_c, v_c = _heads(c_k), _heads(c_v)
    k_all = jnp.concatenate([past_k.astype(f32), k_c], axis=1)
    v_all = jnp.concatenate([past_v.astype(f32), v_c], axis=1)
    o_c = _stick_breaking(_heads(c_q), k_all, v_all, sb_bias[l], past_k.shape[1]).reshape(bsz, t, BR_WIDTH)

    o = jnp.stack([o_a, o_b, o_c], axis=2)
    branch = jnp.einsum('btjr,jrd->btjd', o, w_branch[l].astype(f32))
    gate = jax.nn.sigmoid(g_mix.reshape(bsz, t, N_BRANCH, d))
    mixed = jnp.sum(gate * branch, axis=2)
    x = x + mixed.astype(x.dtype) @ w_out[l]

    h2 = _rmsnorm(x, norm_ffn[l]).reshape(bsz * t, d)
    ffn = _hier_moe(h2, w_router_group[l], b_router_group[l], w_router_expert[l], b_router_expert[l],
                    w_exp_gate[l], w_exp_up[l], w_exp_down[l])
    x = x + ffn.reshape(bsz, t, d).astype(x.dtype)
    return x, s_hgrn_new, s_gdn_new, conv_new, k_c, v_c


def setup_inputs(seed: int = 0) -> dict:
    key = jax.random.key(seed)
    ks = jax.random.split(key, 32)
    f32 = jnp.float32
    n_pages = PAST_LEN // PAGE_SIZE
    n_used = DEC_BATCH * n_pages
    n_pool = n_used + n_used // 4

    def nrm(k, shape, scale):
        return jax.random.normal(k, shape, f32) * scale

    def gain(k, shape):
        return 1.0 + 0.02 * jax.random.normal(k, shape, f32)

    dt = jnp.exp(jax.random.uniform(ks[14], (DEPTH, H_B), f32, math.log(1e-3), math.log(1e-1)))
    page_table = jax.random.permutation(ks[7], n_pool)[:n_used].reshape(DEC_BATCH, n_pages).astype(jnp.int32)
    return {
        'x_prompt': nrm(ks[0], (BATCH, SEQ, D_MODEL), 1.0),
        'x_sample': nrm(ks[1], (DEC_BATCH, DEC_SEQ, D_MODEL), 1.0),
        'state_hgrn': nrm(ks[2], (DEPTH, DEC_BATCH, H_A, HEAD_DIM, HEAD_DIM), 0.5),
        'state_gdn': nrm(ks[3], (DEPTH, DEC_BATCH, H_B, HEAD_DIM, HEAD_DIM), 0.1),
        'state_gdn_conv': nrm(ks[4], (DEPTH, DEC_BATCH, CONV_W - 1, 3 * BR_WIDTH), 1.0),
        'cache_k': nrm(ks[5], (DEPTH, n_pool, PAGE_SIZE, H_C, HEAD_DIM), 1.0),
        'cache_v': nrm(ks[6], (DEPTH, n_pool, PAGE_SIZE, H_C, HEAD_DIM), 1.0),
        'page_table': page_table,
        'norm_mix': gain(ks[8], (DEPTH, D_MODEL)),
        'w_in': nrm(ks[9], (DEPTH, D_MODEL, IN_COLS), D_MODEL ** -0.5),
        'hgrn_lb': nrm(ks[10], (DEPTH, H_A * HEAD_DIM), 0.5),
        'hgrn_norm': gain(ks[11], (DEPTH, HEAD_DIM)),
        'gdn_conv': nrm(ks[12], (DEPTH, CONV_W, 3 * BR_WIDTH), CONV_W ** -0.5),
        'gdn_a_log': jnp.log(jax.random.uniform(ks[13], (DEPTH, H_B), f32, 1.0, 16.0)),
        'gdn_dt_bias': dt + jnp.log(-jnp.expm1(-dt)),
        'gdn_norm': gain(ks[15], (DEPTH, HEAD_DIM)),
        'sb_bias': SB_BIAS_INIT + 0.1 * jax.random.normal(ks[27], (DEPTH, H_C), f32),
        'w_branch': nrm(ks[16], (DEPTH, N_BRANCH, BR_WIDTH, D_MODEL), BR_WIDTH ** -0.5),
        'w_out': nrm(ks[17], (DEPTH, D_MODEL, D_MODEL), D_MODEL ** -0.5),
        'norm_ffn': gain(ks[18], (DEPTH, D_MODEL)),
        'w_router_group': nrm(ks[19], (DEPTH, D_MODEL, N_GROUPS), D_MODEL ** -0.5),
        'b_router_group': nrm(ks[20], (DEPTH, N_GROUPS), 0.01),
        'w_router_expert': nrm(ks[21], (DEPTH, D_MODEL, N_EXPERTS), D_MODEL ** -0.5),
        'b_router_expert': nrm(ks[22], (DEPTH, N_EXPERTS), 0.01),
        'w_exp_gate': nrm(ks[23], (DEPTH, N_EXPERTS, D_MODEL, D_EXPERT), D_MODEL ** -0.5),
        'w_exp_up': nrm(ks[24], (DEPTH, N_EXPERTS, D_MODEL, D_EXPERT), D_MODEL ** -0.5),
        'w_exp_down': nrm(ks[25], (DEPTH, N_EXPERTS, D_EXPERT, D_MODEL), D_EXPERT ** -0.5),
        'final_norm': gain(ks[26], (D_MODEL,)),
    }


def reference(x_prompt, x_sample, state_hgrn, state_gdn, state_gdn_conv, cache_k, cache_v, page_table,
              norm_mix, w_in, hgrn_lb, hgrn_norm, gdn_conv, gdn_a_log, gdn_dt_bias, gdn_norm, sb_bias,
              w_branch, w_out, norm_ffn, w_router_group, b_router_group, w_router_expert, b_router_expert,
              w_exp_gate, w_exp_up, w_exp_down, final_norm):
    f32 = jnp.float32
    weights = (norm_mix, w_in, hgrn_lb, hgrn_norm, gdn_conv, gdn_a_log, gdn_dt_bias, gdn_norm, sb_bias,
               w_branch, w_out, norm_ffn, w_router_group, b_router_group, w_router_expert,
               b_router_expert, w_exp_gate, w_exp_up, w_exp_down)
    bp = x_prompt.shape[0]
    bs = x_sample.shape[0]
    xp, xs = x_prompt, x_sample
    hgrn_p, gdn_p, conv_p, k_p, v_p = [], [], [], [], []
    hgrn_s, gdn_s, conv_s, k_s, v_s = [], [], [], [], []
    for l in range(DEPTH):
        xp, sh, sg, sc, kn, vn = _decoder_layer(
            xp, l,
            jnp.zeros((bp, H_A, HEAD_DIM, HEAD_DIM), f32),
            jnp.zeros((bp, H_B, HEAD_DIM, HEAD_DIM), f32),
            jnp.zeros((bp, CONV_W - 1, 3 * BR_WIDTH), f32),
            jnp.zeros((bp, 0, H_C, HEAD_DIM), f32),
            jnp.zeros((bp, 0, H_C, HEAD_DIM), f32),
            *weights)
        hgrn_p.append(sh); gdn_p.append(sg); conv_p.append(sc); k_p.append(kn); v_p.append(vn)
        past_k = cache_k[l][page_table].reshape(bs, -1, H_C, HEAD_DIM)
        past_v = cache_v[l][page_table].reshape(bs, -1, H_C, HEAD_DIM)
        xs, sh, sg, sc, kn, vn = _decoder_layer(
            xs, l, state_hgrn[l], state_gdn[l], state_gdn_conv[l], past_k, past_v, *weights)
        hgrn_s.append(sh); gdn_s.append(sg); conv_s.append(sc); k_s.append(kn); v_s.append(vn)
    y_prompt = _rmsnorm(xp, final_norm)
    y_sample = _rmsnorm(xs, final_norm)
    return (y_prompt, y_sample,
            jnp.stack(hgrn_p), jnp.stack(gdn_p), jnp.stack(conv_p), jnp.stack(k_p), jnp.stack(v_p),
            jnp.stack(hgrn_s), jnp.stack(gdn_s), jnp.stack(conv_s), jnp.stack(k_s), jnp.stack(v_s))
```

```python
import functools

import jax
import jax.numpy as jnp
from jax import lax
from jax.experimental import pallas as pl
from jax.experimental.pallas import tpu as pltpu

F32 = jnp.float32
BF16 = jnp.bfloat16

HEAD_DIM = 128
CONV_W = 4
CHUNK = 64
N_GROUPS = 4
EXPERTS_PER_GROUP = 8
N_EXPERTS = N_GROUPS * EXPERTS_PER_GROUP
EPS = 1e-6
LANES = 128
SUBLANES = 8
VMEM_LIMIT_BYTES = 56 * 1024 * 1024
NEG_INF = float("-inf")

IN_TILE = 1024
ROW_TILE = 512
NORM_TILE = 256
PROJ_TN = 512
MOE_TILE = 256
SB_TQ = 256
SB_TK = 128


def _params(*sem):
    return pltpu.CompilerParams(dimension_semantics=sem, vmem_limit_bytes=VMEM_LIMIT_BYTES)


def _split_bf16(x, n):
    parts = []
    r = x
    for i in range(n):
        p = r.astype(BF16)
        parts.append(p)
        if i + 1 < n:
            r = r - p.astype(F32)
    return parts


_NN = (((1,), (0,)), ((), ()))
_NT = (((1,), (1,)), ((), ()))
_TN = (((0,), (0,)), ((), ()))


def _dot(a, b, dims=_NN):
    return lax.dot_general(a, b, dims, preferred_element_type=F32)


def _dotb(a, b, dims=_NN):
    return _dot(a.astype(BF16), b.astype(BF16), dims)


def _dot3(a, b, dims=_NN):
    a_hi, a_lo = _split_bf16(a, 2)
    b_hi, b_lo = _split_bf16(b, 2)
    return _dot(a_hi, b_hi, dims) + (_dot(a_hi, b_lo, dims) + _dot(a_lo, b_hi, dims))


def _mm(precise):
    return _dot3 if precise else _dotb


def _dot01_left(m01, x):
    p0, p1, p2 = _split_bf16(x, 3)
    return _dot(m01, p0) + (_dot(m01, p1) + _dot(m01, p2))


def _dot01_right(x, m01):
    p0, p1, p2 = _split_bf16(x, 3)
    return _dot(p0, m01) + (_dot(p1, m01) + _dot(p2, m01))


def _softplus(x):
    return jnp.maximum(x, 0.0) + jnp.log1p(jnp.exp(-jnp.abs(x)))


def _silu(x):
    return x * jax.nn.sigmoid(x)


def _tri(n, strict=False):
    r = lax.broadcasted_iota(jnp.int32, (n, n), 0)
    c = lax.broadcasted_iota(jnp.int32, (n, n), 1)
    return (r > c) if strict else (r >= c)


def _rms(x, w):
    return x * lax.rsqrt(jnp.mean(x * x, axis=-1, keepdims=True) + EPS) * w


def _gated_head_norm(o, w, z):
    return _rms(o, w) * _silu(z)


def _rmsnorm_kernel(x_ref, w_ref, o_ref):
    o_ref[...] = _rms(x_ref[...], w_ref[...]).astype(o_ref.dtype)


def _rmsnorm(x, w, out_dtype, tm):
    n, d = x.shape
    row = pl.BlockSpec((tm, d), lambda i: (i, 0))
    return pl.pallas_call(
        _rmsnorm_kernel, grid=(n // tm,),
        in_specs=[row, pl.BlockSpec((1, d), lambda i: (0, 0))],
        out_specs=row, out_shape=jax.ShapeDtypeStruct((n, d), out_dtype),
        compiler_params=_params("parallel"), name="rmsnorm")(x, w.reshape(1, d))


def _add_norm_kernel(x_ref, y_ref, w_ref, xo_ref, ho_ref):
    x = x_ref[...] + y_ref[...]
    xo_ref[...] = x
    ho_ref[...] = _rms(x, w_ref[...]).astype(ho_ref.dtype)


def _add_norm(x, y, w, out_dtype):
    n, d = x.shape
    return pl.pallas_call(
        _add_norm_kernel,
        out_shape=(jax.ShapeDtypeStruct((n, d), F32), jax.ShapeDtypeStruct((n, d), out_dtype)),
        compiler_params=pltpu.CompilerParams(vmem_limit_bytes=VMEM_LIMIT_BYTES),
        name="add_norm")(x, y, w.reshape(1, d))


def _row_copies(idx_ref, idx0, n_rows, src_hbm, dst_vmem, sem, start):
    def body(r, carry):
        cp = pltpu.make_async_copy(src_hbm.at[pl.ds(idx_ref[idx0 + r], 1), :], dst_vmem.at[pl.ds(r, 1), :], sem)
        if start:
            cp.start()
        else:
            cp.wait()
        return carry

    lax.fori_loop(0, n_rows, body, 0)


def _moe_add_norm_kernel(back_ref, x_ref, y_hbm, w_ref, xo_ref, ho_ref, ybuf, sems):
    i = pl.program_id(0)
    n_i = pl.num_programs(0)
    tm = x_ref.shape[0]
    n_tok = n_i * tm

    def copies(step, start):
        slot = step % 2
        for k in range(2):
            _row_copies(back_ref, k * n_tok + step * tm, tm, y_hbm, ybuf.at[slot, k], sems.at[slot], start)

    @pl.when(i == 0)
    def _():
        copies(i, True)

    @pl.when(i + 1 < n_i)
    def _():
        copies(i + 1, True)

    copies(i, False)
    slot = i % 2
    x = x_ref[...] + (ybuf[slot, 0] + ybuf[slot, 1])
    xo_ref[...] = x
    ho_ref[...] = _rms(x, w_ref[...]).astype(ho_ref.dtype)


def _moe_add_norm(x, y_rows, back, w, out_dtype):
    n, d = x.shape
    tm = NORM_TILE
    row = pl.BlockSpec((tm, d), lambda i, back: (i, 0))
    return pl.pallas_call(
        _moe_add_norm_kernel,
        grid_spec=pltpu.PrefetchScalarGridSpec(
            num_scalar_prefetch=1, grid=(n // tm,),
            in_specs=[row, pl.BlockSpec(memory_space=pl.ANY), pl.BlockSpec((1, d), lambda i, back: (0, 0))],
            out_specs=(row, row),
            scratch_shapes=[pltpu.VMEM((2, 2, tm, d), F32), pltpu.SemaphoreType.DMA((2,))]),
        out_shape=(jax.ShapeDtypeStruct((n, d), F32), jax.ShapeDtypeStruct((n, d), out_dtype)),
        compiler_params=_params("arbitrary"), name="moe_add_norm")(back, x, y_rows, w.reshape(1, d))


def _mm_kernel(a_ref, b_ref, *rest, has_res):
    o_ref = rest[-1]
    out = _dotb(a_ref[...], b_ref[...])
    if has_res:
        out = out + rest[0][...]
    o_ref[...] = out.astype(o_ref.dtype)


def _weight_spec(b, layer, k, tn, col_block):
    if b.ndim == 2:
        return pl.BlockSpec((k, tn), lambda *g: (0, col_block(*g)))
    return pl.BlockSpec((None, k, tn), lambda *g: (layer, 0, col_block(*g)))


def _matmul(a, b, *, tm, tn, layer=None, col0=0, n_cols=None, residual=None, name="matmul"):
    m, k = a.shape
    n_cols = b.shape[-1] if n_cols is None else n_cols
    blk0 = col0 // tn
    in_specs = [pl.BlockSpec((tm, k), lambda i, j: (i, 0)), _weight_spec(b, layer, k, tn, lambda i, j: blk0 + j)]
    args = [a, b]
    if residual is not None:
        in_specs.append(pl.BlockSpec((tm, tn), lambda i, j: (i, j)))
        args.append(residual)
    return pl.pallas_call(
        functools.partial(_mm_kernel, has_res=residual is not None),
        grid=(m // tm, n_cols // tn), in_specs=in_specs,
        out_specs=pl.BlockSpec((tm, tn), lambda i, j: (i, j)),
        out_shape=jax.ShapeDtypeStruct((m, n_cols), F32),
        compiler_params=_params("parallel", "parallel"), name=name)(*args)


def _mm_precise_kernel(a_ref, b0_ref, *rest, shift, has_res):
    o_ref = rest[-1]
    a = a_ref[...]
    out = _dot3(a, b0_ref[...])
    if shift:
        out = jnp.concatenate([out, _dot3(a, rest[0][...])], axis=1)[:, shift:shift + o_ref.shape[1]]
    if has_res:
        out = out + rest[-2][...]
    o_ref[...] = out


def _matmul_precise(a, b, *, tn, layer=None, col0=0, n_cols=None, residual=None, name="matmul_precise"):
    m, k = a.shape
    n_cols = b.shape[-1] if n_cols is None else n_cols
    blk0, shift = divmod(col0, tn)
    in_specs = [pl.BlockSpec((m, k), lambda j: (0, 0)), _weight_spec(b, layer, k, tn, lambda j: blk0 + j)]
    args = [a, b]
    if shift:
        in_specs.append(_weight_spec(b, layer, k, tn, lambda j: blk0 + j + 1))
        args.append(b)
    if residual is not None:
        in_specs.append(pl.BlockSpec((m, tn), lambda j: (0, j)))
        args.append(residual)
    return pl.pallas_call(
        functools.partial(_mm_precise_kernel, shift=shift, has_res=residual is not None),
        grid=(n_cols // tn,), in_specs=in_specs,
        out_specs=pl.BlockSpec((m, tn), lambda j: (0, j)),
        out_shape=jax.ShapeDtypeStruct((m, n_cols), F32),
        compiler_params=_params("parallel"), name=name)(*args)


def _merge_kernel(oa_ref, ob_ref, oc_ref, wb_ref, ga_ref, gb_ref, gc_ref, o_ref, *, precise):
    acc = None
    for j, (o_r, g_r) in enumerate(((oa_ref, ga_ref), (ob_ref, gb_ref), (oc_ref, gc_ref))):
        term = jax.nn.sigmoid(g_r[...]) * _mm(precise)(o_r[...], wb_ref[j])
        acc = term if acc is None else acc + term
    o_ref[...] = acc


def _merge(o_a, o_b, o_c, w_branch, layer, proj, gate_col0, d_model, *, tm, precise):
    n, br = o_a.shape
    tn = PROJ_TN
    g0 = gate_col0 // tn
    gstep = d_model // tn
    o_spec = pl.BlockSpec((tm, br), lambda i, j: (i, 0))
    g_specs = [pl.BlockSpec((tm, tn), functools.partial(lambda i, j, b: (i, g0 + b * gstep + j), b=b))
               for b in range(3)]
    return pl.pallas_call(
        functools.partial(_merge_kernel, precise=precise), grid=(n // tm, d_model // tn),
        in_specs=[o_spec, o_spec, o_spec, pl.BlockSpec((None, 3, br, tn), lambda i, j: (layer, 0, 0, j))] + g_specs,
        out_specs=pl.BlockSpec((tm, tn), lambda i, j: (i, j)),
        out_shape=jax.ShapeDtypeStruct((n, d_model), F32),
        compiler_params=_params("parallel", "parallel"), name="merge")(o_a, o_b, o_c, w_branch, proj, proj, proj)


def _hgrn_kernel(q_ref, f_ref, i_ref, g_ref, llb_ref, l1m_ref, oml_ref, nw_ref, s0_ref,
                 o_ref, sfin_ref, st_scr, o_scr, qkvb_scr, *, chunk, precise):
    c = pl.program_id(2)
    n_c = pl.num_programs(2)
    mm = _mm(precise)
    operand = (lambda x: x) if precise else (lambda x: x.astype(BF16).astype(F32))

    @pl.when(c == 0)
    def _():
        st_scr[...] = s0_ref[0, 0].T

    af = f_ref[...]
    log_sig = jnp.minimum(af, 0.0) - jnp.log1p(jnp.exp(-jnp.abs(af)))
    a = llb_ref[...]
    b2 = l1m_ref[...] + log_sig
    log_f = jnp.maximum(a, b2) + jnp.log1p(jnp.exp(-jnp.abs(a - b2)))
    k = oml_ref[...] * jax.nn.sigmoid(-af)
    q = _silu(q_ref[...])
    v = i_ref[...]
    b = _dot01_left(_tri(chunk).astype(BF16), log_f)
    st = st_scr[...]
    o_scr[...] = mm(q * jnp.exp(b), st, _NT)
    qkvb_scr[0] = q
    qkvb_scr[1] = k
    qkvb_scr[2] = operand(v)
    qkvb_scr[3] = b
    for s in range(chunk):
        r0 = (s // SUBLANES) * SUBLANES
        rows = r0 + lax.broadcasted_iota(jnp.int32, (chunk - r0, 1), 0)
        d = jnp.where(rows >= s, qkvb_scr[3, r0:, :] - qkvb_scr[3, s:s + 1, :], NEG_INF)
        col = jnp.sum(qkvb_scr[0, r0:, :] * qkvb_scr[1, s:s + 1, :] * jnp.exp(d), axis=-1, keepdims=True)
        o_scr[r0:, :] += operand(col) * qkvb_scr[2, s:s + 1, :]
    o_ref[...] = _gated_head_norm(o_scr[...], nw_ref[...], g_ref[...])
    b_end = b[chunk - 1:chunk, :]
    st_new = st * jnp.exp(b_end) + mm(v, k * jnp.exp(b_end - b), _TN)
    st_scr[...] = st_new

    @pl.when(c == n_c - 1)
    def _():
        sfin_ref[0, 0] = st_new.T


def _hgrn(proj, lb_params, norm_w, s0, *, seq, chunk, precise):
    bsz, n_h = s0.shape[:2]
    n_c = seq // chunk

    def col(seg):
        return pl.BlockSpec((chunk, HEAD_DIM), lambda b, h, c: (b * n_c + c, seg * n_h + h))

    par = pl.BlockSpec((1, HEAD_DIM), lambda b, h, c: (0, h))
    state = pl.BlockSpec((1, 1, HEAD_DIM, HEAD_DIM), lambda b, h, c: (b, h, 0, 0))
    in_specs = [col(0), col(1), col(2), col(3), par, par, par,
                pl.BlockSpec((1, HEAD_DIM), lambda b, h, c: (0, 0)), state]
    args = [proj, proj, proj, proj, *lb_params, norm_w.reshape(1, HEAD_DIM), s0]
    return pl.pallas_call(
        functools.partial(_hgrn_kernel, chunk=chunk, precise=precise),
        grid=(bsz, n_h, n_c), in_specs=in_specs,
        out_specs=(pl.BlockSpec((chunk, HEAD_DIM), lambda b, h, c: (b * n_c + c, h)), state),
        out_shape=(jax.ShapeDtypeStruct((bsz * seq, n_h * HEAD_DIM), F32),
                   jax.ShapeDtypeStruct(s0.shape, F32)),
        scratch_shapes=[pltpu.VMEM((HEAD_DIM, HEAD_DIM), F32), pltpu.VMEM((chunk, HEAD_DIM), F32),
                        pltpu.VMEM((4, chunk, HEAD_DIM), F32)],
        compiler_params=_params("parallel", "parallel", "arbitrary"), name="hgrn")(*args)


def _gdn_kernel(q_ref, k_ref, v_ref, z_ref, ab_ref, cwq_ref, cwk_ref, cwv_ref, cbq_ref, cbk_ref, cbv_ref,
                gp_ref, nw_ref, s0_ref, o_ref, sfin_ref, s_scr, xp_scr, x_scr, *, chunk, precise):
    h = pl.program_id(1)
    c = pl.program_id(2)
    n_c = pl.num_programs(2)
    mm = _mm(precise)
    hist = CONV_W - 1
    base = SUBLANES - hist

    @pl.when(c == 0)
    def _():
        s_scr[...] = s0_ref[0, 0]
        for j, cb in enumerate((cbq_ref, cbk_ref, cbv_ref)):
            xp_scr[j, base:SUBLANES, :] = cb[0]

    def conv(j, x_r, cw_r):
        xp_scr[j, SUBLANES:SUBLANES + chunk, :] = x_r[...]
        y = xp_scr[j, base:base + chunk, :] * cw_r[0:1, :]
        for t in range(1, CONV_W):
            y = y + xp_scr[j, base + t:base + t + chunk, :] * cw_r[t:t + 1, :]
        xp_scr[j, base:SUBLANES, :] = xp_scr[j, base + chunk:SUBLANES + chunk, :]
        return _silu(y)

    qc = conv(0, q_ref, cwq_ref)
    kc = conv(1, k_ref, cwk_ref)
    vc = conv(2, v_ref, cwv_ref)
    qn = qc * lax.rsqrt(jnp.sum(qc * qc, axis=-1, keepdims=True) + EPS) * (HEAD_DIM ** -0.5)
    kn = kc * lax.rsqrt(jnp.sum(kc * kc, axis=-1, keepdims=True) + EPS)

    ab = ab_ref[...]
    n_h = pl.num_programs(1)
    g_all = -jnp.exp(gp_ref[0:1, :]) * _softplus(ab + gp_ref[1:2, :])
    cg_all = _dot01_left(_tri(chunk).astype(BF16), g_all)
    lane = lax.broadcasted_iota(jnp.int32, (1, LANES), 1)
    cg = jnp.sum(jnp.where(lane == h, cg_all, 0.0), axis=1, keepdims=True)
    beta = jnp.sum(jnp.where(lane == h + n_h, jax.nn.sigmoid(ab), 0.0), axis=1, keepdims=True)
    incl = _tri(chunk)
    strict = _tri(chunk, strict=True)
    eye = jnp.logical_and(incl, jnp.logical_not(strict))
    cg_row = jnp.sum(jnp.where(eye, jnp.broadcast_to(cg, (chunk, chunk)), 0.0), axis=0, keepdims=True)
    gam = jnp.exp(jnp.where(incl, cg - cg_row, NEG_INF))
    lower = jnp.where(strict, beta * gam * mm(kn, kn, _NT), 0.0)
    s = s_scr[...]
    e_cg = jnp.exp(cg)
    x_scr[...] = beta * (vc - e_cg * mm(kn, s))
    for j in range(chunk - 1):
        r0 = (j // SUBLANES) * SUBLANES
        x_scr[r0:, :] -= lower[r0:, j:j + 1] * x_scr[j:j + 1, :]
    u = x_scr[...]
    qk = mm(qn, kn, _NT) * gam
    o = e_cg * mm(qn, s) + mm(qk, u)
    cg_end = cg[chunk - 1:chunk, :]
    s_new = jnp.exp(cg_end) * s + mm(kn * jnp.exp(cg_end - cg), u, _TN)
    s_scr[...] = s_new
    o_ref[...] = _gated_head_norm(o, nw_ref[...], z_ref[...])

    @pl.when(c == n_c - 1)
    def _():
        sfin_ref[0, 0] = s_new


def _gdn(proj, proj_ab, conv_w, conv_buf, gate_params, norm_w, s0, *, seq, chunk, precise):
    bsz, n_h = s0.shape[:2]
    n_c = seq // chunk

    def col(seg):
        return pl.BlockSpec((chunk, HEAD_DIM), lambda b, h, c: (b * n_c + c, seg * n_h + h))

    def cw(seg):
        return pl.BlockSpec((CONV_W, HEAD_DIM), lambda b, h, c: (0, seg * n_h + h))

    def cb(seg):
        return pl.BlockSpec((1, CONV_W - 1, HEAD_DIM), lambda b, h, c: (b, 0, seg * n_h + h))

    state = pl.BlockSpec((1, 1, HEAD_DIM, HEAD_DIM), lambda b, h, c: (b, h, 0, 0))
    in_specs = [col(4), col(5), col(6), col(7),
                pl.BlockSpec((chunk, LANES), lambda b, h, c: (b * n_c + c, 0)),
                cw(0), cw(1), cw(2), cb(0), cb(1), cb(2),
                pl.BlockSpec((2, LANES), lambda b, h, c: (0, 0)),
                pl.BlockSpec((1, HEAD_DIM), lambda b, h, c: (0, 0)), state]
    args = [proj, proj, proj, proj, proj_ab, conv_w, conv_w, conv_w, conv_buf, conv_buf, conv_buf,
            gate_params, norm_w.reshape(1, HEAD_DIM), s0]
    return pl.pallas_call(
        functools.partial(_gdn_kernel, chunk=chunk, precise=precise),
        grid=(bsz, n_h, n_c), in_specs=in_specs,
        out_specs=(pl.BlockSpec((chunk, HEAD_DIM), lambda b, h, c: (b * n_c + c, h)), state),
        out_shape=(jax.ShapeDtypeStruct((bsz * seq, n_h * HEAD_DIM), F32),
                   jax.ShapeDtypeStruct(s0.shape, F32)),
        scratch_shapes=[pltpu.VMEM((HEAD_DIM, HEAD_DIM), F32),
                        pltpu.VMEM((3, SUBLANES + chunk, HEAD_DIM), F32),
                        pltpu.VMEM((chunk, HEAD_DIM), F32)],
        compiler_params=_params("parallel", "parallel", "arbitrary"), name="gdn")(*args)


def _sb_block(z, mask, run, u_incl):
    sp = _softplus(z)
    log_stay = jnp.where(mask, -sp, 0.0)
    incl = _dot01_right(log_stay, u_incl)
    later = incl - log_stay + run
    w = jnp.where(mask, jnp.exp((z - sp) + later), 0.0)
    return w, run + incl[:, 0:1]


def _sb_prompt_kernel(bias_ref, q_ref, k_ref, v_ref, o_ref, acc_scr, run_scr):
    h = pl.program_id(1)
    qi = pl.program_id(2)
    tq = q_ref.shape[0]
    q = q_ref[...].astype(BF16)
    bias = bias_ref[h]
    scale = HEAD_DIM ** -0.5
    acc_scr[...] = jnp.zeros_like(acc_scr)
    run_scr[...] = jnp.zeros_like(run_scr)
    t_pos = qi * tq + lax.broadcasted_iota(jnp.int32, (tq, 1), 0)
    u_incl = _tri(SB_TK).astype(BF16)
    n_kb = (qi + 1) * (tq // SB_TK)

    def body(i, carry):
        k0 = pl.multiple_of((n_kb - 1 - i) * SB_TK, SB_TK)
        kb = k_ref[pl.ds(k0, SB_TK), :].astype(BF16)
        vb = v_ref[pl.ds(k0, SB_TK), :].astype(BF16)
        z = _dot(q, kb, _NT) * scale + bias
        s_pos = k0 + lax.broadcasted_iota(jnp.int32, (1, SB_TK), 1)
        w, run = _sb_block(z, s_pos < t_pos, run_scr[...], u_incl)
        acc_scr[...] += _dot(w.astype(BF16), vb)
        run_scr[...] = run
        return carry

    lax.fori_loop(0, n_kb, body, 0)
    o_ref[...] = acc_scr[...]


def _sb_prompt(proj, bias, *, bsz, seq, n_h):
    n_q = seq // SB_TQ
    kv = lambda seg: pl.BlockSpec((seq, HEAD_DIM), lambda b, h, i, bias: (b, seg * n_h + h))
    return pl.pallas_call(
        _sb_prompt_kernel,
        grid_spec=pltpu.PrefetchScalarGridSpec(
            num_scalar_prefetch=1, grid=(bsz, n_h, n_q),
            in_specs=[pl.BlockSpec((SB_TQ, HEAD_DIM), lambda b, h, i, bias: (b * n_q + i, 8 * n_h + h)),
                      kv(9), kv(10)],
            out_specs=pl.BlockSpec((SB_TQ, HEAD_DIM), lambda b, h, i, bias: (b * n_q + i, h)),
            scratch_shapes=[pltpu.VMEM((SB_TQ, HEAD_DIM), F32), pltpu.VMEM((SB_TQ, 1), F32)]),
        out_shape=jax.ShapeDtypeStruct((bsz * seq, n_h * HEAD_DIM), F32),
        compiler_params=_params("parallel", "parallel", "arbitrary"), name="sb_prompt")(bias, proj, proj, proj)


def _sb_sample_kernel(pt_ref, bias_ref, q_ref, ko_ref, vo_ref, kp_ref, vp_ref, o_ref,
                      acc_scr, run_scr, kown_scr, vown_scr, *, n_h):
    del pt_ref
    j = pl.program_id(1)
    n_j = pl.num_programs(1)
    t = q_ref.shape[0]
    scale = HEAD_DIM ** -0.5
    u_incl = _tri(SB_TK).astype(BF16)

    def attend(k_of, v_of, mask):
        zs = []
        for h in range(n_h):
            qh = q_ref[:, h * HEAD_DIM:(h + 1) * HEAD_DIM]
            zs.append(_dot3(qh, k_of(h), _NT) * scale + bias_ref[h])
        z = jnp.concatenate(zs, axis=0)
        w, run = _sb_block(z, mask, run_scr[...], u_incl)
        run_scr[...] = run
        for h in range(n_h):
            acc_scr[h] += _dot3(w[h * t:(h + 1) * t, :], v_of(h))

    @pl.when(j == 0)
    def _():
        acc_scr[...] = jnp.zeros_like(acc_scr)
        run_scr[...] = jnp.zeros_like(run_scr)
        kown_scr[...] = jnp.zeros_like(kown_scr)
        vown_scr[...] = jnp.zeros_like(vown_scr)
        kown_scr[0:t, :] = ko_ref[...]
        vown_scr[0:t, :] = vo_ref[...]
        rows = jnp.concatenate([lax.broadcasted_iota(jnp.int32, (t, 1), 0)] * n_h, axis=0)
        cols = lax.broadcasted_iota(jnp.int32, (1, SB_TK), 1)
        attend(lambda h: kown_scr[:, h * HEAD_DIM:(h + 1) * HEAD_DIM],
               lambda h: vown_scr[:, h * HEAD_DIM:(h + 1) * HEAD_DIM], cols < rows)

    @pl.when(j > 0)
    def _():
        attend(lambda h: kp_ref[:, h, :], lambda h: vp_ref[:, h, :],
               jnp.full((n_h * t, SB_TK), True))

    @pl.when(j == n_j - 1)
    def _():
        for h in range(n_h):
            o_ref[:, h * HEAD_DIM:(h + 1) * HEAD_DIM] = acc_scr[h]


def _sb_sample(proj, cache_k, cache_v, layer, page_table, bias, *, bsz, seq, n_h):
    n_pages = page_table.shape[1]
    br = n_h * HEAD_DIM

    def own(seg):
        return pl.BlockSpec((seq, br), lambda b, j, pt, bias: (b, seg))

    def page_index(b, j, pt, bias):
        return (layer, pt[b, n_pages - jnp.maximum(j, 1)], 0, 0, 0)

    page = pl.BlockSpec((None, None, cache_k.shape[2], n_h, HEAD_DIM), page_index)
    return pl.pallas_call(
        functools.partial(_sb_sample_kernel, n_h=n_h),
        grid_spec=pltpu.PrefetchScalarGridSpec(
            num_scalar_prefetch=2, grid=(bsz, n_pages + 1),
            in_specs=[own(8), own(9), own(10), page, page],
            out_specs=pl.BlockSpec((seq, br), lambda b, j, pt, bias: (b, 0)),
            scratch_shapes=[pltpu.VMEM((n_h, seq, HEAD_DIM), F32), pltpu.VMEM((n_h * seq, 1), F32),
                            pltpu.VMEM((SB_TK, br), F32), pltpu.VMEM((SB_TK, br), F32)]),
        out_shape=jax.ShapeDtypeStruct((bsz * seq, br), F32),
        compiler_params=_params("parallel", "arbitrary"), name="sb_sample")(
            page_table, bias, proj, proj, proj, cache_k, cache_v)


def _router_kernel(x_ref, nw_ref, wr_ref, br_ref, h_ref, r_ref, *, precise):
    hn = _rms(x_ref[...], nw_ref[...])
    h_ref[...] = hn
    logits = _mm(precise)(hn, wr_ref[...]) + br_ref[...]
    lane = lax.broadcasted_iota(jnp.int32, logits.shape, 1)
    big = jnp.int32(LANES)

    def first_max(mask):
        m = jnp.max(jnp.where(mask, logits, NEG_INF), axis=-1, keepdims=True)
        idx = jnp.min(jnp.where(jnp.logical_and(mask, logits == m), lane, big), axis=-1, keepdims=True)
        return m, idx

    is_grp = lane < N_GROUPS
    g_max, grp = first_max(is_grp)
    p_grp = 1.0 / jnp.sum(jnp.where(is_grp, jnp.exp(logits - g_max), 0.0), axis=-1, keepdims=True)
    e_lo = N_GROUPS + grp * EXPERTS_PER_GROUP
    in_grp = jnp.logical_and(lane >= e_lo, lane < e_lo + EXPERTS_PER_GROUP)
    v1, i1 = first_max(in_grp)
    v2, i2 = first_max(jnp.logical_and(in_grp, lane != i1))
    e2 = jnp.exp(v2 - v1)
    w1 = (1.0 / (1.0 + e2)) * p_grp
    w2 = (e2 / (1.0 + e2)) * p_grp
    out = jnp.where(lane == 0, (i1 - N_GROUPS).astype(F32), 0.0)
    out = jnp.where(lane == 1, (i2 - N_GROUPS).astype(F32), out)
    out = jnp.where(lane == 2, w1, out)
    out = jnp.where(lane == 3, w2, out)
    r_ref[...] = out


def _router(x, norm_w, w_router, b_router, *, tm, precise):
    n, d = x.shape
    row = pl.BlockSpec((tm, d), lambda i: (i, 0))
    return pl.pallas_call(
        functools.partial(_router_kernel, precise=precise), grid=(n // tm,),
        in_specs=[row, pl.BlockSpec((1, d), lambda i: (0, 0)), pl.BlockSpec((d, LANES), lambda i: (0, 0)),
                  pl.BlockSpec((1, LANES), lambda i: (0, 0))],
        out_specs=(row, pl.BlockSpec((tm, LANES), lambda i: (i, 0))),
        out_shape=(jax.ShapeDtypeStruct((n, d), F32), jax.ShapeDtypeStruct((n, LANES), F32)),
        compiler_params=_params("parallel"), name="router")(x, norm_w.reshape(1, d), w_router, b_router)


def _expert_up_kernel(src_ref, te_ref, nt_ref, x_hbm, wg_ref, wu_ref, rw_ref, o_ref, xbuf, sems):
    del te_ref
    i = pl.program_id(0)
    n_t = nt_ref[0]
    tm = o_ref.shape[0]

    def copies(step, start):
        slot = step % 2
        _row_copies(src_ref, step * tm, tm, x_hbm, xbuf.at[slot], sems.at[slot], start)

    @pl.when(i == 0)
    def _():
        copies(i, True)

    @pl.when(i + 1 < n_t)
    def _():
        copies(i + 1, True)

    @pl.when(i < n_t)
    def _():
        copies(i, False)
        x = xbuf[i % 2].astype(BF16)
        g = _dot(x, wg_ref[...].astype(BF16))
        u = _dot(x, wu_ref[...].astype(BF16))
        o_ref[...] = ((_silu(g) * u) * rw_ref[...]).astype(o_ref.dtype)

    @pl.when(i >= n_t)
    def _():
        o_ref[...] = jnp.zeros_like(o_ref)


def _expert_down_kernel(te_ref, nt_ref, h_ref, wd_ref, o_ref):
    del te_ref

    @pl.when(pl.program_id(0) < nt_ref[0])
    def _():
        o_ref[...] = _dot(h_ref[...], wd_ref[...].astype(BF16))

    @pl.when(pl.program_id(0) >= nt_ref[0])
    def _():
        o_ref[...] = jnp.zeros_like(o_ref)


def _experts(h2, src_tok, row_w, tile_expert, n_tiles, w_gate, w_up, w_down, layer):
    r = src_tok.shape[0]
    d = h2.shape[1]
    f = w_gate.shape[-1]
    n_t = r // MOE_TILE
    hid = pl.pallas_call(
        _expert_up_kernel,
        grid_spec=pltpu.PrefetchScalarGridSpec(
            num_scalar_prefetch=3, grid=(n_t,),
            in_specs=[pl.BlockSpec(memory_space=pl.ANY),
                      pl.BlockSpec((None, None, d, f), lambda i, src, te, nt: (layer, te[i], 0, 0)),
                      pl.BlockSpec((None, None, d, f), lambda i, src, te, nt: (layer, te[i], 0, 0)),
                      pl.BlockSpec((MOE_TILE, 1), lambda i, src, te, nt: (i, 0))],
            out_specs=pl.BlockSpec((MOE_TILE, f), lambda i, src, te, nt: (i, 0)),
            scratch_shapes=[pltpu.VMEM((2, MOE_TILE, d), F32), pltpu.SemaphoreType.DMA((2,))]),
        out_shape=jax.ShapeDtypeStruct((r, f), BF16),
        compiler_params=_params("arbitrary"), name="expert_up")(
            src_tok, tile_expert, n_tiles, h2, w_gate, w_up, row_w)
    return pl.pallas_call(
        _expert_down_kernel,
        grid_spec=pltpu.PrefetchScalarGridSpec(
            num_scalar_prefetch=2, grid=(n_t,),
            in_specs=[pl.BlockSpec((MOE_TILE, f), lambda i, te, nt: (i, 0)),
                      pl.BlockSpec((None, None, f, d), lambda i, te, nt: (layer, te[i], 0, 0))],
            out_specs=pl.BlockSpec((MOE_TILE, d), lambda i, te, nt: (i, 0))),
        out_shape=jax.ShapeDtypeStruct((r, d), F32),
        compiler_params=_params("arbitrary"), name="expert_down")(tile_expert, n_tiles, hid, w_down)


def _moe_schedule(eid, wsel):
    n = eid.shape[0]
    flat_e = eid.reshape(-1)
    onehot = (flat_e[:, None] == jnp.arange(N_EXPERTS, dtype=jnp.int32)[None, :]).astype(jnp.int32)
    rank = jnp.sum((jnp.cumsum(onehot, axis=0) - onehot) * onehot, axis=1)
    counts = jnp.sum(onehot, axis=0)
    tiles_e = (counts + MOE_TILE - 1) // MOE_TILE
    tiles_end = jnp.cumsum(tiles_e)
    row_start = (tiles_end - tiles_e) * MOE_TILE
    dest = row_start[flat_e] + rank
    n_tiles_max = (2 * n + MOE_TILE - 1) // MOE_TILE + N_EXPERTS
    n_rows = n_tiles_max * MOE_TILE
    src_tok = jnp.zeros((n_rows,), jnp.int32).at[dest].set(jnp.arange(2 * n, dtype=jnp.int32) // 2)
    row_w = jnp.zeros((n_rows,), F32).at[dest].set(wsel.reshape(-1))
    n_tiles = tiles_end[-1]
    tile_ids = jnp.minimum(jnp.arange(n_tiles_max, dtype=jnp.int32), n_tiles - 1)
    tile_expert = jnp.searchsorted(tiles_end, tile_ids, side="right").astype(jnp.int32)
    return dest, src_tok, row_w.reshape(n_rows, 1), tile_expert, n_tiles.reshape(1).astype(jnp.int32)


def _moe_sorted(x, norm_w, w_router, b_router, w_gate, w_up, w_down, layer):
    n = x.shape[0]
    h2, routed = _router(x, norm_w, w_router, b_router, tm=NORM_TILE, precise=False)
    eid = routed[:, 0:2].astype(jnp.int32)
    wsel = routed[:, 2:4]
    dest, src_tok, row_w, tile_expert, n_tiles = _moe_schedule(eid, wsel)
    y_rows = _experts(h2, src_tok, row_w, tile_expert, n_tiles, w_gate, w_up, w_down, layer)
    back = dest.reshape(n, 2).T.reshape(-1)
    return y_rows, back


def _experts_dense_kernel(x_ref, cw_ref, wg_ref, wu_ref, wd_ref, o_ref):
    e = pl.program_id(0)
    fi = pl.program_id(1)

    @pl.when(jnp.logical_and(e == 0, fi == 0))
    def _():
        o_ref[...] = jnp.zeros_like(o_ref)

    x = x_ref[...]
    lane = lax.broadcasted_iota(jnp.int32, (1, LANES), 1)
    cw = jnp.sum(jnp.where(lane == e, cw_ref[...], 0.0), axis=1, keepdims=True)
    hid = (_silu(_dot3(x, wg_ref[...])) * _dot3(x, wu_ref[...])) * cw
    o_ref[...] += _dot3(hid, wd_ref[...])


def _moe_dense(x, norm_w, w_router, b_router, w_gate, w_up, w_down, layer):
    n, d = x.shape
    f = w_gate.shape[-1]
    tf = f // 2
    h2, routed = _router(x, norm_w, w_router, b_router, tm=n, precise=True)
    eid = routed[:, 0:2].astype(jnp.int32)
    combine = jnp.sum(jax.nn.one_hot(eid, LANES, dtype=F32) * routed[:, 2:4, None], axis=1)
    full = lambda shape: pl.BlockSpec(shape, lambda e, fi: (0, 0))
    return pl.pallas_call(
        _experts_dense_kernel, grid=(N_EXPERTS, f // tf),
        in_specs=[full((n, d)), full((n, LANES)),
                  pl.BlockSpec((None, None, d, tf), lambda e, fi: (layer, e, 0, fi)),
                  pl.BlockSpec((None, None, d, tf), lambda e, fi: (layer, e, 0, fi)),
                  pl.BlockSpec((None, None, tf, d), lambda e, fi: (layer, e, fi, 0))],
        out_specs=full((n, d)), out_shape=jax.ShapeDtypeStruct((n, d), F32),
        compiler_params=_params("arbitrary", "arbitrary"), name="experts_dense")(
            h2, combine, w_gate, w_up, w_down)


def kernel(x_prompt, x_sample, state_hgrn, state_gdn, state_gdn_conv, cache_k, cache_v, page_table, norm_mix, w_in, hgrn_lb, hgrn_norm, gdn_conv, gdn_a_log, gdn_dt_bias, gdn_norm, sb_bias, w_branch, w_out, norm_ffn, w_router_group, b_router_group, w_router_expert, b_router_expert, w_exp_gate, w_exp_up, w_exp_down, final_norm):
    bp, tp, d = x_prompt.shape
    bs, ts, _ = x_sample.shape
    depth = w_in.shape[0]
    n_h = state_hgrn.shape[2]
    br = n_h * HEAD_DIM
    n_p, n_s = bp * tp, bs * ts
    assert n_p % IN_TILE == 0 and n_p % ROW_TILE == 0 and n_p % NORM_TILE == 0
    assert tp % CHUNK == 0 and tp % SB_TQ == 0 and ts % SUBLANES == 0 and n_s % SUBLANES == 0
    assert 2 * n_h <= LANES and cache_k.shape[2] == SB_TK and ts <= SB_TK and br % PROJ_TN == 0

    ab0 = 8 * br
    ab1 = ab0 + 2 * n_h
    n_main = 11 * br + 3 * d
    gate_col0 = 11 * br
    lb_all = jnp.cumsum(jax.nn.softmax(hgrn_lb.astype(F32), axis=0), axis=0)
    zeros_h = jnp.zeros((bp, n_h, HEAD_DIM, HEAD_DIM), F32)
    zeros_conv = jnp.zeros((bp, CONV_W - 1, 3 * br), F32)

    xp = x_prompt.reshape(n_p, d)
    xs = x_sample.reshape(n_s, d)
    outs_p = [[] for _ in range(5)]
    outs_s = [[] for _ in range(5)]
    hp = _rmsnorm(xp, norm_mix[0], BF16, ROW_TILE)
    hs = _rmsnorm(xs, norm_mix[0], F32, n_s)
    for l in range(depth):
        lb = (lb_all[l] - lb_all[0]).reshape(1, br)
        lb_params = (jnp.log(lb), jnp.log1p(-lb), 1.0 - lb)
        gate_params = jnp.pad(jnp.stack([gdn_a_log[l], gdn_dt_bias[l]]).astype(F32), ((0, 0), (0, LANES - n_h)))
        conv_w = gdn_conv[l].astype(F32)
        bias = sb_bias[l].astype(F32)
        w_router = jnp.pad(jnp.concatenate([w_router_group[l], w_router_expert[l]], axis=1),
                           ((0, 0), (0, LANES - N_GROUPS - N_EXPERTS)))
        b_router = jnp.pad(jnp.concatenate([b_router_group[l], b_router_expert[l]]).astype(F32),
                           (0, LANES - N_GROUPS - N_EXPERTS)).reshape(1, LANES)
        next_norm = norm_mix[l + 1] if l + 1 < depth else final_norm

        w_main = jnp.concatenate([w_in[l, :, :ab0], w_in[l, :, ab1:]], axis=1).astype(BF16)
        proj = _matmul(hp, w_main, tm=IN_TILE, tn=PROJ_TN, name="in_proj")
        proj_ab = _matmul(hp, w_in, layer=l, tm=IN_TILE, tn=LANES, col0=ab0, n_cols=LANES, name="in_proj_gates")
        o_a, hg_p = _hgrn(proj, lb_params, hgrn_norm[l], zeros_h, seq=tp, chunk=CHUNK, precise=False)
        o_b, gd_p = _gdn(proj, proj_ab, conv_w, zeros_conv, gate_params, gdn_norm[l], zeros_h,
                         seq=tp, chunk=CHUNK, precise=False)
        o_c = _sb_prompt(proj, bias, bsz=bp, seq=tp, n_h=n_h)
        mixed = _merge(o_a, o_b, o_c, w_branch, l, proj, gate_col0, d, tm=ROW_TILE, precise=False)
        xp = _matmul(mixed, w_out, layer=l, tm=ROW_TILE, tn=PROJ_TN, residual=xp, name="out_proj")
        y_rows, back = _moe_sorted(xp, norm_ffn[l], w_router, b_router, w_exp_gate, w_exp_up, w_exp_down, l)
        xp, hp = _moe_add_norm(xp, y_rows, back, next_norm, BF16 if l + 1 < depth else F32)
        outs_p[0].append(hg_p)
        outs_p[1].append(gd_p)
        outs_p[2].append(proj[:, 4 * br:7 * br].reshape(bp, tp, 3 * br)[:, tp - (CONV_W - 1):])
        outs_p[3].append(proj[:, 9 * br:10 * br].reshape(bp, tp, n_h, HEAD_DIM))
        outs_p[4].append(proj[:, 10 * br:11 * br].reshape(bp, tp, n_h, HEAD_DIM))

        proj_s = jnp.concatenate([
            _matmul_precise(hs, w_in, layer=l, tn=PROJ_TN, col0=0, n_cols=ab0, name="in_proj_s0"),
            _matmul_precise(hs, w_in, layer=l, tn=PROJ_TN, col0=ab1, n_cols=n_main - ab0, name="in_proj_s1")],
            axis=1)
        proj_ab_s = _matmul_precise(hs, w_in, layer=l, tn=LANES, col0=ab0, n_cols=LANES, name="in_proj_gates_s")
        oa_s, hg_s = _hgrn(proj_s, lb_params, hgrn_norm[l], state_hgrn[l].astype(F32), seq=ts, chunk=ts,
                           precise=True)
        ob_s, gd_s = _gdn(proj_s, proj_ab_s, conv_w, state_gdn_conv[l].astype(F32), gate_params, gdn_norm[l],
                          state_gdn[l].astype(F32), seq=ts, chunk=ts, precise=True)
        oc_s = _sb_sample(proj_s, cache_k, cache_v, l, page_table, bias, bsz=bs, seq=ts, n_h=n_h)
        mixed_s = _merge(oa_s, ob_s, oc_s, w_branch, l, proj_s, gate_col0, d, tm=n_s, precise=True)
        xs = _matmul_precise(mixed_s, w_out, layer=l, tn=PROJ_TN, residual=xs, name="out_proj_s")
        y_s = _moe_dense(xs, norm_ffn[l], w_router, b_router, w_exp_gate, w_exp_up, w_exp_down, l)
        xs, hs = _add_norm(xs, y_s, next_norm, F32)
        outs_s[0].append(hg_s)
        outs_s[1].append(gd_s)
        outs_s[2].append(proj_s[:, 4 * br:7 * br].reshape(bs, ts, 3 * br)[:, ts - (CONV_W - 1):])
        outs_s[3].append(proj_s[:, 9 * br:10 * br].reshape(bs, ts, n_h, HEAD_DIM))
        outs_s[4].append(proj_s[:, 10 * br:11 * br].reshape(bs, ts, n_h, HEAD_DIM))

    y_prompt = hp.reshape(bp, tp, d)
    y_sample = hs.reshape(bs, ts, d)
    return (y_prompt, y_sample, *(jnp.stack(o) for o in outs_p), *(jnp.stack(o) for o in outs_s))
```

```python
import functools

import jax
import jax.numpy as jnp
from jax import lax
from jax.experimental import pallas as pl
from jax.experimental.pallas import tpu as pltpu

F32 = jnp.float32
BF16 = jnp.bfloat16

HEAD_DIM = 128
CONV_W = 4
CHUNK = 64
N_GROUPS = 4
EXPERTS_PER_GROUP = 8
N_EXPERTS = N_GROUPS * EXPERTS_PER_GROUP
EPS = 1e-6
LANES = 128
SUBLANES = 8
VMEM_LIMIT_BYTES = 56 * 1024 * 1024
NEG_INF = float("-inf")

IN_TILE = 1024
ROW_TILE = 512
NORM_TILE = 256
PROJ_TN = 512
MOE_TILE = 256
SB_TQ = 256
SB_TK = 128
HEADS_PER_STEP = 8
SB_HEADS_PER_STEP = 4


def _params(*sem):
    return pltpu.CompilerParams(dimension_semantics=sem, vmem_limit_bytes=VMEM_LIMIT_BYTES)


def _split_bf16(x, n):
    parts = []
    r = x
    for i in range(n):
        p = r.astype(BF16)
        parts.append(p)
        if i + 1 < n:
            r = r - p.astype(F32)
    return parts


_NN = (((1,), (0,)), ((), ()))
_NT = (((1,), (1,)), ((), ()))
_TN = (((0,), (0,)), ((), ()))


def _dot(a, b, dims=_NN):
    return lax.dot_general(a, b, dims, preferred_element_type=F32)


def _dotb(a, b, dims=_NN):
    return _dot(a.astype(BF16), b.astype(BF16), dims)


def _dot3(a, b, dims=_NN):
    a_hi, a_lo = _split_bf16(a, 2)
    b_hi, b_lo = _split_bf16(b, 2)
    return _dot(a_hi, b_hi, dims) + (_dot(a_hi, b_lo, dims) + _dot(a_lo, b_hi, dims))


def _mm(precise):
    return _dot3 if precise else _dotb


def _dot01_left(m01, x):
    p0, p1, p2 = _split_bf16(x, 3)
    return _dot(m01, p0) + (_dot(m01, p1) + _dot(m01, p2))


def _dot01_right(x, m01):
    p0, p1, p2 = _split_bf16(x, 3)
    return _dot(p0, m01) + (_dot(p1, m01) + _dot(p2, m01))


def _softplus(x):
    return jnp.maximum(x, 0.0) + jnp.log1p(jnp.exp(-jnp.abs(x)))


def _silu(x):
    return x * jax.nn.sigmoid(x)


def _tri(n, strict=False):
    r = lax.broadcasted_iota(jnp.int32, (n, n), 0)
    c = lax.broadcasted_iota(jnp.int32, (n, n), 1)
    return (r > c) if strict else (r >= c)


def _rms(x, w):
    return x * lax.rsqrt(jnp.mean(x * x, axis=-1, keepdims=True) + EPS) * w


def _gated_head_norm(o, w, z):
    return _rms(o, w) * _silu(z)


def _rmsnorm_kernel(x_ref, w_ref, o_ref):
    o_ref[...] = _rms(x_ref[...], w_ref[...]).astype(o_ref.dtype)


def _rmsnorm(x, w, out_dtype, tm):
    n, d = x.shape
    row = pl.BlockSpec((tm, d), lambda i: (i, 0))
    return pl.pallas_call(
        _rmsnorm_kernel, grid=(n // tm,),
        in_specs=[row, pl.BlockSpec((1, d), lambda i: (0, 0))],
        out_specs=row, out_shape=jax.ShapeDtypeStruct((n, d), out_dtype),
        compiler_params=_params("parallel"), name="rmsnorm")(x, w.reshape(1, d))


def _add_norm_kernel(x_ref, y_ref, w_ref, xo_ref, ho_ref):
    x = x_ref[...] + y_ref[...]
    xo_ref[...] = x
    ho_ref[...] = _rms(x, w_ref[...]).astype(ho_ref.dtype)


def _add_norm(x, y, w, out_dtype):
    n, d = x.shape
    return pl.pallas_call(
        _add_norm_kernel,
        out_shape=(jax.ShapeDtypeStruct((n, d), F32), jax.ShapeDtypeStruct((n, d), out_dtype)),
        compiler_params=pltpu.CompilerParams(vmem_limit_bytes=VMEM_LIMIT_BYTES),
        name="add_norm")(x, y, w.reshape(1, d))


def _row_copies(idx_ref, idx0, n_rows, src_hbm, dst_vmem, sem, start):
    def body(r, carry):
        cp = pltpu.make_async_copy(src_hbm.at[pl.ds(idx_ref[idx0 + r], 1), :], dst_vmem.at[pl.ds(r, 1), :], sem)
        if start:
            cp.start()
        else:
            cp.wait()
        return carry

    lax.fori_loop(0, n_rows, body, 0)


def _moe_add_norm_kernel(back_ref, x_ref, y_hbm, w_ref, xo_ref, ho_ref, ybuf, sems):
    i = pl.program_id(0)
    n_i = pl.num_programs(0)
    tm = x_ref.shape[0]
    n_tok = n_i * tm

    def copies(step, start):
        slot = step % 2
        for k in range(2):
            _row_copies(back_ref, k * n_tok + step * tm, tm, y_hbm, ybuf.at[slot, k], sems.at[slot], start)

    @pl.when(i == 0)
    def _():
        copies(i, True)

    @pl.when(i + 1 < n_i)
    def _():
        copies(i + 1, True)

    copies(i, False)
    slot = i % 2
    x = x_ref[...] + (ybuf[slot, 0] + ybuf[slot, 1])
    xo_ref[...] = x
    ho_ref[...] = _rms(x, w_ref[...]).astype(ho_ref.dtype)


def _moe_add_norm(x, y_rows, back, w, out_dtype):
    n, d = x.shape
    tm = NORM_TILE
    row = pl.BlockSpec((tm, d), lambda i, back: (i, 0))
    return pl.pallas_call(
        _moe_add_norm_kernel,
        grid_spec=pltpu.PrefetchScalarGridSpec(
            num_scalar_prefetch=1, grid=(n // tm,),
            in_specs=[row, pl.BlockSpec(memory_space=pl.ANY), pl.BlockSpec((1, d), lambda i, back: (0, 0))],
            out_specs=(row, row),
            scratch_shapes=[pltpu.VMEM((2, 2, tm, d), F32), pltpu.SemaphoreType.DMA((2,))]),
        out_shape=(jax.ShapeDtypeStruct((n, d), F32), jax.ShapeDtypeStruct((n, d), out_dtype)),
        compiler_params=_params("arbitrary"), name="moe_add_norm")(back, x, y_rows, w.reshape(1, d))


def _mm_kernel(a_ref, b_ref, *rest, has_res):
    o_ref = rest[-1]
    out = _dotb(a_ref[...], b_ref[...])
    if has_res:
        out = out + rest[0][...]
    o_ref[...] = out.astype(o_ref.dtype)


def _weight_spec(b, layer, k, tn, col_block):
    if b.ndim == 2:
        return pl.BlockSpec((k, tn), lambda *g: (0, col_block(*g)))
    return pl.BlockSpec((None, k, tn), lambda *g: (layer, 0, col_block(*g)))


def _matmul(a, b, *, tm, tn, layer=None, col0=0, n_cols=None, residual=None, name="matmul"):
    m, k = a.shape
    n_cols = b.shape[-1] if n_cols is None else n_cols
    blk0 = col0 // tn
    in_specs = [pl.BlockSpec((tm, k), lambda i, j: (i, 0)), _weight_spec(b, layer, k, tn, lambda i, j: blk0 + j)]
    args = [a, b]
    if residual is not None:
        in_specs.append(pl.BlockSpec((tm, tn), lambda i, j: (i, j)))
        args.append(residual)
    return pl.pallas_call(
        functools.partial(_mm_kernel, has_res=residual is not None),
        grid=(m // tm, n_cols // tn), in_specs=in_specs,
        out_specs=pl.BlockSpec((tm, tn), lambda i, j: (i, j)),
        out_shape=jax.ShapeDtypeStruct((m, n_cols), F32),
        compiler_params=_params("parallel", "parallel"), name=name)(*args)


def _mm_precise_kernel(a_ref, b0_ref, *rest, shift, has_res):
    o_ref = rest[-1]
    a = a_ref[...]
    out = _dot3(a, b0_ref[...])
    if shift:
        out = jnp.concatenate([out, _dot3(a, rest[0][...])], axis=1)[:, shift:shift + o_ref.shape[1]]
    if has_res:
        out = out + rest[-2][...]
    o_ref[...] = out


def _matmul_precise(a, b, *, tn, layer=None, col0=0, n_cols=None, residual=None, name="matmul_precise"):
    m, k = a.shape
    n_cols = b.shape[-1] if n_cols is None else n_cols
    blk0, shift = divmod(col0, tn)
    in_specs = [pl.BlockSpec((m, k), lambda j: (0, 0)), _weight_spec(b, layer, k, tn, lambda j: blk0 + j)]
    args = [a, b]
    if shift:
        in_specs.append(_weight_spec(b, layer, k, tn, lambda j: blk0 + j + 1))
        args.append(b)
    if residual is not None:
        in_specs.append(pl.BlockSpec((m, tn), lambda j: (0, j)))
        args.append(residual)
    return pl.pallas_call(
        functools.partial(_mm_precise_kernel, shift=shift, has_res=residual is not None),
        grid=(n_cols // tn,), in_specs=in_specs,
        out_specs=pl.BlockSpec((m, tn), lambda j: (0, j)),
        out_shape=jax.ShapeDtypeStruct((m, n_cols), F32),
        compiler_params=_params("parallel"), name=name)(*args)


def _merge_kernel(oa_ref, ob_ref, oc_ref, wb_ref, ga_ref, gb_ref, gc_ref, o_ref, *, precise):
    acc = None
    for j, (o_r, g_r) in enumerate(((oa_ref, ga_ref), (ob_ref, gb_ref), (oc_ref, gc_ref))):
        term = jax.nn.sigmoid(g_r[...]) * _mm(precise)(o_r[...], wb_ref[j])
        acc = term if acc is None else acc + term
    o_ref[...] = acc


def _merge(o_a, o_b, o_c, w_branch, layer, proj, gate_col0, d_model, *, tm, precise):
    n, br = o_a.shape
    tn = PROJ_TN
    g0 = gate_col0 // tn
    gstep = d_model // tn
    o_spec = pl.BlockSpec((tm, br), lambda i, j: (i, 0))
    g_specs = [pl.BlockSpec((tm, tn), functools.partial(lambda i, j, b: (i, g0 + b * gstep + j), b=b))
               for b in range(3)]
    return pl.pallas_call(
        functools.partial(_merge_kernel, precise=precise), grid=(n // tm, d_model // tn),
        in_specs=[o_spec, o_spec, o_spec, pl.BlockSpec((None, 3, br, tn), lambda i, j: (layer, 0, 0, j))] + g_specs,
        out_specs=pl.BlockSpec((tm, tn), lambda i, j: (i, j)),
        out_shape=jax.ShapeDtypeStruct((n, d_model), F32),
        compiler_params=_params("parallel", "parallel"), name="merge")(o_a, o_b, o_c, w_branch, proj, proj, proj)


def _hgrn_kernel(q_ref, f_ref, i_ref, g_ref, llb_ref, l1m_ref, oml_ref, nw_ref, s0_ref,
                 o_ref, sfin_ref, *scratch, chunk, precise, hp):
    c = pl.program_id(2)
    n_c = pl.num_programs(2)
    mm = _mm(precise)
    operand = (lambda x: x) if precise else (lambda x: x.astype(BF16).astype(F32))
    st_scrs, o_scrs, qkvb_scrs = scratch[:hp], scratch[hp:2 * hp], scratch[2 * hp:]
    lanes = [slice(hh * HEAD_DIM, (hh + 1) * HEAD_DIM) for hh in range(hp)]

    @pl.when(c == 0)
    def _():
        for hh in range(hp):
            st_scrs[hh][...] = s0_ref[0, hh].T

    kept = []
    for hh in range(hp):
        af = f_ref[:, lanes[hh]]
        log_sig = jnp.minimum(af, 0.0) - jnp.log1p(jnp.exp(-jnp.abs(af)))
        a = llb_ref[:, lanes[hh]]
        b2 = l1m_ref[:, lanes[hh]] + log_sig
        log_f = jnp.maximum(a, b2) + jnp.log1p(jnp.exp(-jnp.abs(a - b2)))
        k = oml_ref[:, lanes[hh]] * jax.nn.sigmoid(-af)
        q = _silu(q_ref[:, lanes[hh]])
        v = i_ref[:, lanes[hh]]
        b = _dot01_left(_tri(chunk).astype(BF16), log_f)
        st = st_scrs[hh][...]
        o_scrs[hh][...] = mm(q * jnp.exp(b), st, _NT)
        qkvb_scrs[hh][0] = q
        qkvb_scrs[hh][1] = k
        qkvb_scrs[hh][2] = operand(v)
        qkvb_scrs[hh][3] = b
        kept.append((k, v, b, st))
    for s in range(chunk):
        r0 = (s // SUBLANES) * SUBLANES
        rows = r0 + lax.broadcasted_iota(jnp.int32, (chunk - r0, 1), 0)
        for hh in range(hp):
            qkvb, o_scr = qkvb_scrs[hh], o_scrs[hh]
            d = jnp.where(rows >= s, qkvb[3, r0:, :] - qkvb[3, s:s + 1, :], NEG_INF)
            col = jnp.sum(qkvb[0, r0:, :] * qkvb[1, s:s + 1, :] * jnp.exp(d), axis=-1, keepdims=True)
            o_scr[r0:, :] += operand(col) * qkvb[2, s:s + 1, :]
    for hh in range(hp):
        k, v, b, st = kept[hh]
        o_ref[:, lanes[hh]] = _gated_head_norm(o_scrs[hh][...], nw_ref[...], g_ref[:, lanes[hh]])
        b_end = b[chunk - 1:chunk, :]
        st_new = st * jnp.exp(b_end) + mm(v, k * jnp.exp(b_end - b), _TN)
        st_scrs[hh][...] = st_new

        @pl.when(c == n_c - 1)
        def _(hh=hh, st_new=st_new):
            sfin_ref[0, hh] = st_new.T


def _hgrn(proj, lb_params, norm_w, s0, *, seq, chunk, precise, hp):
    bsz, n_h = s0.shape[:2]
    n_c = seq // chunk
    n_hg = n_h // hp
    wide = hp * HEAD_DIM

    def col(seg):
        return pl.BlockSpec((chunk, wide), lambda b, h, c: (b * n_c + c, seg * n_hg + h))

    par = pl.BlockSpec((1, wide), lambda b, h, c: (0, h))
    state = pl.BlockSpec((1, hp, HEAD_DIM, HEAD_DIM), lambda b, h, c: (b, h, 0, 0))
    in_specs = [col(0), col(1), col(2), col(3), par, par, par,
                pl.BlockSpec((1, HEAD_DIM), lambda b, h, c: (0, 0)), state]
    args = [proj, proj, proj, proj, *lb_params, norm_w.reshape(1, HEAD_DIM), s0]
    return pl.pallas_call(
        functools.partial(_hgrn_kernel, chunk=chunk, precise=precise, hp=hp),
        grid=(bsz, n_hg, n_c), in_specs=in_specs,
        out_specs=(pl.BlockSpec((chunk, wide), lambda b, h, c: (b * n_c + c, h)), state),
        out_shape=(jax.ShapeDtypeStruct((bsz * seq, n_h * HEAD_DIM), F32),
                   jax.ShapeDtypeStruct(s0.shape, F32)),
        scratch_shapes=([pltpu.VMEM((HEAD_DIM, HEAD_DIM), F32)] * hp + [pltpu.VMEM((chunk, HEAD_DIM), F32)] * hp
                        + [pltpu.VMEM((4, chunk, HEAD_DIM), F32)] * hp),
        compiler_params=_params("parallel", "parallel", "arbitrary"), name="hgrn")(*args)


def _gdn_kernel(q_ref, k_ref, v_ref, z_ref, ab_ref, cwq_ref, cwk_ref, cwv_ref, cbq_ref, cbk_ref, cbv_ref,
                gp_ref, nw_ref, s0_ref, o_ref, sfin_ref, *scratch, chunk, precise, hp, n_h):
    hg = pl.program_id(1)
    c = pl.program_id(2)
    n_c = pl.num_programs(2)
    mm = _mm(precise)
    hist = CONV_W - 1
    base = SUBLANES - hist
    s_scrs, xp_scrs, x_scrs = scratch[:hp], scratch[hp:2 * hp], scratch[2 * hp:]
    lanes = [slice(hh * HEAD_DIM, (hh + 1) * HEAD_DIM) for hh in range(hp)]

    @pl.when(c == 0)
    def _():
        for hh in range(hp):
            s_scrs[hh][...] = s0_ref[0, hh]
            for j, cb in enumerate((cbq_ref, cbk_ref, cbv_ref)):
                xp_scrs[hh][j, base:SUBLANES, :] = cb[0, :, lanes[hh]]

    def conv(hh, j, x_r, cw_r):
        xp = xp_scrs[hh]
        xp[j, SUBLANES:SUBLANES + chunk, :] = x_r[:, lanes[hh]]
        y = xp[j, base:base + chunk, :] * cw_r[0:1, lanes[hh]]
        for t in range(1, CONV_W):
            y = y + xp[j, base + t:base + t + chunk, :] * cw_r[t:t + 1, lanes[hh]]
        xp[j, base:SUBLANES, :] = xp[j, base + chunk:SUBLANES + chunk, :]
        return _silu(y)

    ab = ab_ref[...]
    g_all = -jnp.exp(gp_ref[0:1, :]) * _softplus(ab + gp_ref[1:2, :])
    cg_all = _dot01_left(_tri(chunk).astype(BF16), g_all)
    sig_ab = jax.nn.sigmoid(ab)
    lane = lax.broadcasted_iota(jnp.int32, (1, LANES), 1)
    incl = _tri(chunk)
    strict = _tri(chunk, strict=True)
    eye = jnp.logical_and(incl, jnp.logical_not(strict))

    kept = []
    for hh in range(hp):
        h = hg * hp + hh
        qc = conv(hh, 0, q_ref, cwq_ref)
        kc = conv(hh, 1, k_ref, cwk_ref)
        vc = conv(hh, 2, v_ref, cwv_ref)
        qn = qc * lax.rsqrt(jnp.sum(qc * qc, axis=-1, keepdims=True) + EPS) * (HEAD_DIM ** -0.5)
        kn = kc * lax.rsqrt(jnp.sum(kc * kc, axis=-1, keepdims=True) + EPS)
        cg = jnp.sum(jnp.where(lane == h, cg_all, 0.0), axis=1, keepdims=True)
        beta = jnp.sum(jnp.where(lane == h + n_h, sig_ab, 0.0), axis=1, keepdims=True)
        cg_row = jnp.sum(jnp.where(eye, jnp.broadcast_to(cg, (chunk, chunk)), 0.0), axis=0, keepdims=True)
        gam = jnp.exp(jnp.where(incl, cg - cg_row, NEG_INF))
        lower = jnp.where(strict, beta * gam * mm(kn, kn, _NT), 0.0)
        s = s_scrs[hh][...]
        e_cg = jnp.exp(cg)
        x_scrs[hh][...] = beta * (vc - e_cg * mm(kn, s))
        kept.append((qn, kn, cg, gam, lower, s, e_cg))
    for j in range(chunk - 1):
        r0 = (j // SUBLANES) * SUBLANES
        for hh in range(hp):
            x_scr, lower = x_scrs[hh], kept[hh][4]
            x_scr[r0:, :] -= lower[r0:, j:j + 1] * x_scr[j:j + 1, :]
    for hh in range(hp):
        qn, kn, cg, gam, lower, s, e_cg = kept[hh]
        u = x_scrs[hh][...]
        qk = mm(qn, kn, _NT) * gam
        o = e_cg * mm(qn, s) + mm(qk, u)
        cg_end = cg[chunk - 1:chunk, :]
        s_new = jnp.exp(cg_end) * s + mm(kn * jnp.exp(cg_end - cg), u, _TN)
        s_scrs[hh][...] = s_new
        o_ref[:, lanes[hh]] = _gated_head_norm(o, nw_ref[...], z_ref[:, lanes[hh]])

        @pl.when(c == n_c - 1)
        def _(hh=hh, s_new=s_new):
            sfin_ref[0, hh] = s_new


def _gdn(proj, proj_ab, conv_w, conv_buf, gate_params, norm_w, s0, *, seq, chunk, precise, hp):
    bsz, n_h = s0.shape[:2]
    n_c = seq // chunk
    n_hg = n_h // hp
    wide = hp * HEAD_DIM

    def col(seg):
        return pl.BlockSpec((chunk, wide), lambda b, h, c: (b * n_c + c, seg * n_hg + h))

    def cw(seg):
        return pl.BlockSpec((CONV_W, wide), lambda b, h, c: (0, seg * n_hg + h))

    def cb(seg):
        return pl.BlockSpec((1, CONV_W - 1, wide), lambda b, h, c: (b, 0, seg * n_hg + h))

    state = pl.BlockSpec((1, hp, HEAD_DIM, HEAD_DIM), lambda b, h, c: (b, h, 0, 0))
    in_specs = [col(4), col(5), col(6), col(7),
                pl.BlockSpec((chunk, LANES), lambda b, h, c: (b * n_c + c, 0)),
                cw(0), cw(1), cw(2), cb(0), cb(1), cb(2),
                pl.BlockSpec((2, LANES), lambda b, h, c: (0, 0)),
                pl.BlockSpec((1, HEAD_DIM), lambda b, h, c: (0, 0)), state]
    args = [proj, proj, proj, proj, proj_ab, conv_w, conv_w, conv_w, conv_buf, conv_buf, conv_buf,
            gate_params, norm_w.reshape(1, HEAD_DIM), s0]
    return pl.pallas_call(
        functools.partial(_gdn_kernel, chunk=chunk, precise=precise, hp=hp, n_h=n_h),
        grid=(bsz, n_hg, n_c), in_specs=in_specs,
        out_specs=(pl.BlockSpec((chunk, wide), lambda b, h, c: (b * n_c + c, h)), state),
        out_shape=(jax.ShapeDtypeStruct((bsz * seq, n_h * HEAD_DIM), F32),
                   jax.ShapeDtypeStruct(s0.shape, F32)),
        scratch_shapes=([pltpu.VMEM((HEAD_DIM, HEAD_DIM), F32)] * hp
                        + [pltpu.VMEM((3, SUBLANES + chunk, HEAD_DIM), F32)] * hp
                        + [pltpu.VMEM((chunk, HEAD_DIM), F32)] * hp),
        compiler_params=_params("parallel", "parallel", "arbitrary"), name="gdn")(*args)


def _sb_block(z, mask, run, u_incl):
    sp = _softplus(z)
    log_stay = jnp.where(mask, -sp, 0.0)
    incl = _dot01_right(log_stay, u_incl)
    later = incl - log_stay + run
    w = jnp.where(mask, jnp.exp((z - sp) + later), 0.0)
    return w, run + incl[:, 0:1]


def _sb_prompt_kernel(bias_ref, q_ref, k_ref, v_ref, o_ref, *scratch, hp):
    hg = pl.program_id(1)
    qi = pl.program_id(2)
    tq = q_ref.shape[0]
    acc_scrs, run_scrs = scratch[:hp], scratch[hp:]
    lanes = [slice(hh * HEAD_DIM, (hh + 1) * HEAD_DIM) for hh in range(hp)]
    qs = [q_ref[:, lanes[hh]].astype(BF16) for hh in range(hp)]
    biases = [bias_ref[hg * hp + hh] for hh in range(hp)]
    scale = HEAD_DIM ** -0.5
    for hh in range(hp):
        acc_scrs[hh][...] = jnp.zeros_like(acc_scrs[hh])
        run_scrs[hh][...] = jnp.zeros_like(run_scrs[hh])
    t_pos = qi * tq + lax.broadcasted_iota(jnp.int32, (tq, 1), 0)
    u_incl = _tri(SB_TK).astype(BF16)
    n_kb = (qi + 1) * (tq // SB_TK)

    def body(i, carry):
        k0 = pl.multiple_of((n_kb - 1 - i) * SB_TK, SB_TK)
        s_pos = k0 + lax.broadcasted_iota(jnp.int32, (1, SB_TK), 1)
        mask = s_pos < t_pos
        for hh in range(hp):
            kb = k_ref[pl.ds(k0, SB_TK), lanes[hh]].astype(BF16)
            vb = v_ref[pl.ds(k0, SB_TK), lanes[hh]].astype(BF16)
            z = _dot(qs[hh], kb, _NT) * scale + biases[hh]
            w, run = _sb_block(z, mask, run_scrs[hh][...], u_incl)
            acc_scrs[hh][...] += _dot(w.astype(BF16), vb)
            run_scrs[hh][...] = run
        return carry

    lax.fori_loop(0, n_kb, body, 0)
    for hh in range(hp):
        o_ref[:, lanes[hh]] = acc_scrs[hh][...]


def _sb_prompt(proj, bias, *, bsz, seq, n_h, hp):
    n_q = seq // SB_TQ
    n_hg = n_h // hp
    wide = hp * HEAD_DIM
    kv = lambda seg: pl.BlockSpec((seq, wide), lambda b, h, i, bias: (b, seg * n_hg + h))
    return pl.pallas_call(
        functools.partial(_sb_prompt_kernel, hp=hp),
        grid_spec=pltpu.PrefetchScalarGridSpec(
            num_scalar_prefetch=1, grid=(bsz, n_hg, n_q),
            in_specs=[pl.BlockSpec((SB_TQ, wide), lambda b, h, i, bias: (b * n_q + i, 8 * n_hg + h)),
                      kv(9), kv(10)],
            out_specs=pl.BlockSpec((SB_TQ, wide), lambda b, h, i, bias: (b * n_q + i, h)),
            scratch_shapes=[pltpu.VMEM((SB_TQ, HEAD_DIM), F32)] * hp + [pltpu.VMEM((SB_TQ, 1), F32)] * hp),
        out_shape=jax.ShapeDtypeStruct((bsz * seq, n_h * HEAD_DIM), F32),
        compiler_params=_params("parallel", "parallel", "arbitrary"), name="sb_prompt")(bias, proj, proj, proj)


def _sb_sample_kernel(pt_ref, bias_ref, q_ref, ko_ref, vo_ref, kp_ref, vp_ref, o_ref,
                      acc_scr, run_scr, qall_scr, kown_scr, vown_scr, *, n_h):
    del pt_ref
    j = pl.program_id(1)
    n_j = pl.num_programs(1)
    t = q_ref.shape[0]
    n_pg = kp_ref.shape[0]
    t_bits = t.bit_length() - 1
    scale = HEAD_DIM ** -0.5
    lane = lax.broadcasted_iota(jnp.int32, (1, LANES), 1)
    lane_h = lax.shift_right_logical(lane, t_bits)
    lane_t = jnp.bitwise_and(lane, t - 1)
    bias_row = jnp.zeros((1, LANES), F32)
    for h in range(n_h):
        bias_row = jnp.where(lane_h == h, bias_ref[h], bias_row)

    def head_slice(h):
        return slice(h * HEAD_DIM, (h + 1) * HEAD_DIM)

    @pl.when(j == 0)
    def _():
        qall_scr[...] = jnp.zeros_like(qall_scr)
        kown_scr[...] = jnp.zeros_like(kown_scr)
        vown_scr[...] = jnp.zeros_like(vown_scr)
        for h in range(n_h):
            qall_scr[h * t:(h + 1) * t, :] = q_ref[:, head_slice(h)]
            kown_scr[h * t:(h + 1) * t, :] = ko_ref[:, head_slice(h)]
            vown_scr[h * t:(h + 1) * t, :] = vo_ref[:, head_slice(h)]
        z = _dot3(kown_scr[...], qall_scr[...], _NT) * scale + bias_row
        row = lax.broadcasted_iota(jnp.int32, (LANES, 1), 0)
        row_h = lax.shift_right_logical(row, t_bits)
        row_s = jnp.bitwise_and(row, t - 1)
        valid = jnp.logical_and(jnp.logical_and(row_h == lane_h, row_s < lane_t), lane < n_h * t)
        sp = _softplus(z)
        log_stay = jnp.where(valid, -sp, 0.0)
        later_keys =jnp.logical_and(row_h == lane_h, lane_t >= row_s)
        incl = _dot01_left(later_keys.astype(BF16), log_stay)
        w = jnp.where(valid, jnp.exp((z - sp) + (incl - log_stay)), 0.0)
        acc_scr[...] = _dot3(w.T, vown_scr[...])
        run_scr[...] = jnp.sum(log_stay, axis=0, keepdims=True)

    @pl.when(j > 0)
    def _():
        k2 = kp_ref[...].reshape(n_pg * n_h, HEAD_DIM)
        v2 = vp_ref[...].reshape(n_pg * n_h, HEAD_DIM)
        z = _dot3(k2, qall_scr[...], _NT) * scale + bias_row
        row_h = jnp.bitwise_and(lax.broadcasted_iota(jnp.int32, (n_pg * n_h, 1), 0), n_h - 1)
        valid = row_h == lane_h
        sp = _softplus(z)
        log_stay = jnp.where(valid, -sp, 0.0)
        by_key = log_stay.reshape(n_pg, n_h, LANES)
        tail = jnp.zeros((n_h, LANES), F32)
        incl = [None] * n_pg
        for s in reversed(range(n_pg)):
            tail = tail + by_key[s]
            incl[s] = tail
        incl = jnp.stack(incl).reshape(n_pg * n_h, LANES)
        w = jnp.where(valid, jnp.exp((z - sp) + (incl - log_stay + run_scr[...])), 0.0)
        acc_scr[...] += _dot3(w.T, v2)
        run_scr[...] += jnp.sum(tail, axis=0, keepdims=True)

    @pl.when(j == n_j - 1)
    def _():
        for h in range(n_h):
            o_ref[:, head_slice(h)] = acc_scr[h * t:(h + 1) * t, :]


def _sb_sample(proj, cache_k, cache_v, layer, page_table, bias, *, bsz, seq, n_h):
    n_pages = page_table.shape[1]
    br = n_h * HEAD_DIM

    def own(seg):
        return pl.BlockSpec((seq, br), lambda b, j, pt, bias: (b, seg))

    def page_index(b, j, pt, bias):
        return (layer, pt[b, n_pages - jnp.maximum(j, 1)], 0, 0, 0)

    page = pl.BlockSpec((None, None, cache_k.shape[2], n_h, HEAD_DIM), page_index)
    return pl.pallas_call(
        functools.partial(_sb_sample_kernel, n_h=n_h),
        grid_spec=pltpu.PrefetchScalarGridSpec(
            num_scalar_prefetch=2, grid=(bsz, n_pages + 1),
            in_specs=[own(8), own(9), own(10), page, page],
            out_specs=pl.BlockSpec((seq, br), lambda b, j, pt, bias: (b, 0)),
            scratch_shapes=[pltpu.VMEM((LANES, HEAD_DIM), F32), pltpu.VMEM((1, LANES), F32)]
            + [pltpu.VMEM((LANES, HEAD_DIM), F32)] * 3),
        out_shape=jax.ShapeDtypeStruct((bsz * seq, br), F32),
        compiler_params=_params("parallel", "arbitrary"), name="sb_sample")(
            page_table, bias, proj, proj, proj, cache_k, cache_v)


def _router_kernel(x_ref, nw_ref, wr_ref, br_ref, h_ref, r_ref, *, precise):
    hn = _rms(x_ref[...], nw_ref[...])
    h_ref[...] = hn
    logits = _mm(precise)(hn, wr_ref[...]) + br_ref[...]
    lane = lax.broadcasted_iota(jnp.int32, logits.shape, 1)
    big = jnp.int32(LANES)

    def first_max(mask):
        m = jnp.max(jnp.where(mask, logits, NEG_INF), axis=-1, keepdims=True)
        idx = jnp.min(jnp.where(jnp.logical_and(mask, logits == m), lane, big), axis=-1, keepdims=True)
        return m, idx

    is_grp = lane < N_GROUPS
    g_max, grp = first_max(is_grp)
    p_grp = 1.0 / jnp.sum(jnp.where(is_grp, jnp.exp(logits - g_max), 0.0), axis=-1, keepdims=True)
    e_lo = N_GROUPS + grp * EXPERTS_PER_GROUP
    in_grp = jnp.logical_and(lane >= e_lo, lane < e_lo + EXPERTS_PER_GROUP)
    v1, i1 = first_max(in_grp)
    v2, i2 = first_max(jnp.logical_and(in_grp, lane != i1))
    e2 = jnp.exp(v2 - v1)
    w1 = (1.0 / (1.0 + e2)) * p_grp
    w2 = (e2 / (1.0 + e2)) * p_grp
    out = jnp.where(lane == 0, (i1 - N_GROUPS).astype(F32), 0.0)
    out = jnp.where(lane == 1, (i2 - N_GROUPS).astype(F32), out)
    out = jnp.where(lane == 2, w1, out)
    out = jnp.where(lane == 3, w2, out)
    r_ref[...] = out


def _router(x, norm_w, w_router, b_router, *, tm, precise):
    n, d = x.shape
    row = pl.BlockSpec((tm, d), lambda i: (i, 0))
    return pl.pallas_call(
        functools.partial(_router_kernel, precise=precise), grid=(n // tm,),
        in_specs=[row, pl.BlockSpec((1, d), lambda i: (0, 0)), pl.BlockSpec((d, LANES), lambda i: (0, 0)),
                  pl.BlockSpec((1, LANES), lambda i: (0, 0))],
        out_specs=(row, pl.BlockSpec((tm, LANES), lambda i: (i, 0))),
        out_shape=(jax.ShapeDtypeStruct((n, d), F32), jax.ShapeDtypeStruct((n, LANES), F32)),
        compiler_params=_params("parallel"), name="router")(x, norm_w.reshape(1, d), w_router, b_router)


def _expert_up_kernel(src_ref, te_ref, nt_ref, x_hbm, wg_ref, wu_ref, rw_ref, o_ref, xbuf, sems):
    del te_ref
    i = pl.program_id(0)
    n_t = nt_ref[0]
    tm = o_ref.shape[0]

    def copies(step, start):
        slot = step % 2
        _row_copies(src_ref, step * tm, tm, x_hbm, xbuf.at[slot], sems.at[slot], start)

    @pl.when(i == 0)
    def _():
        copies(i, True)

    @pl.when(i + 1 < n_t)
    def _():
        copies(i + 1, True)

    @pl.when(i < n_t)
    def _():
        copies(i, False)
        x = xbuf[i % 2].astype(BF16)
        g = _dot(x, wg_ref[...].astype(BF16))
        u = _dot(x, wu_ref[...].astype(BF16))
        o_ref[...] = ((_silu(g) * u) * rw_ref[...]).astype(o_ref.dtype)

    @pl.when(i >= n_t)
    def _():
        o_ref[...] = jnp.zeros_like(o_ref)


def _expert_down_kernel(te_ref, nt_ref, h_ref, wd_ref, o_ref):
    del te_ref

    @pl.when(pl.program_id(0) < nt_ref[0])
    def _():
        o_ref[...] = _dot(h_ref[...], wd_ref[...].astype(BF16))

    @pl.when(pl.program_id(0) >= nt_ref[0])
    def _():
        o_ref[...] = jnp.zeros_like(o_ref)


def _experts(h2, src_tok, row_w, tile_expert, n_tiles, w_gate, w_up, w_down, layer):
    r = src_tok.shape[0]
    d = h2.shape[1]
    f = w_gate.shape[-1]
    n_t = r // MOE_TILE
    hid = pl.pallas_call(
        _expert_up_kernel,
        grid_spec=pltpu.PrefetchScalarGridSpec(
            num_scalar_prefetch=3, grid=(n_t,),
            in_specs=[pl.BlockSpec(memory_space=pl.ANY),
                      pl.BlockSpec((None, None, d, f), lambda i, src, te, nt: (layer, te[i], 0, 0)),
                      pl.BlockSpec((None, None, d, f), lambda i, src, te, nt: (layer, te[i], 0, 0)),
                      pl.BlockSpec((MOE_TILE, 1), lambda i, src, te, nt: (i, 0))],
            out_specs=pl.BlockSpec((MOE_TILE, f), lambda i, src, te, nt: (i, 0)),
            scratch_shapes=[pltpu.VMEM((2, MOE_TILE, d), F32), pltpu.SemaphoreType.DMA((2,))]),
        out_shape=jax.ShapeDtypeStruct((r, f), BF16),
        compiler_params=_params("arbitrary"), name="expert_up")(
            src_tok, tile_expert, n_tiles, h2, w_gate, w_up, row_w)
    return pl.pallas_call(
        _expert_down_kernel,
        grid_spec=pltpu.PrefetchScalarGridSpec(
            num_scalar_prefetch=2, grid=(n_t,),
            in_specs=[pl.BlockSpec((MOE_TILE, f), lambda i, te, nt: (i, 0)),
                      pl.BlockSpec((None, None, f, d), lambda i, te, nt: (layer, te[i], 0, 0))],
            out_specs=pl.BlockSpec((MOE_TILE, d), lambda i, te, nt: (i, 0))),
        out_shape=jax.ShapeDtypeStruct((r, d), F32),
        compiler_params=_params("arbitrary"), name="expert_down")(tile_expert, n_tiles, hid, w_down)


def _moe_schedule(eid, wsel):
    n = eid.shape[0]
    flat_e = eid.reshape(-1)
    onehot = (flat_e[:, None] == jnp.arange(N_EXPERTS, dtype=jnp.int32)[None, :]).astype(jnp.int32)
    rank = jnp.sum((jnp.cumsum(onehot, axis=0) - onehot) * onehot, axis=1)
    counts = jnp.sum(onehot, axis=0)
    tiles_e = (counts + MOE_TILE - 1) // MOE_TILE
    tiles_end = jnp.cumsum(tiles_e)
    row_start = (tiles_end - tiles_e) * MOE_TILE
    dest = row_start[flat_e] + rank
    n_tiles_max = (2 * n + MOE_TILE - 1) // MOE_TILE + N_EXPERTS
    n_rows = n_tiles_max * MOE_TILE
    src_tok = jnp.zeros((n_rows,), jnp.int32).at[dest].set(jnp.arange(2 * n, dtype=jnp.int32) // 2)
    row_w = jnp.zeros((n_rows,), F32).at[dest].set(wsel.reshape(-1))
    n_tiles = tiles_end[-1]
    tile_ids = jnp.minimum(jnp.arange(n_tiles_max, dtype=jnp.int32), n_tiles - 1)
    tile_expert = jnp.sum((tile_ids[:, None] >= tiles_end[None, :]).astype(jnp.int32), axis=1)
    return dest, src_tok, row_w.reshape(n_rows, 1), tile_expert, n_tiles.reshape(1).astype(jnp.int32)


def _moe_sorted(x, norm_w, w_router, b_router, w_gate, w_up, w_down, layer):
    n = x.shape[0]
    h2, routed = _router(x, norm_w, w_router, b_router, tm=NORM_TILE, precise=False)
    eid = routed[:, 0:2].astype(jnp.int32)
    wsel = routed[:, 2:4]
    dest, src_tok, row_w, tile_expert, n_tiles = _moe_schedule(eid, wsel)
    y_rows = _experts(h2, src_tok, row_w, tile_expert, n_tiles, w_gate, w_up, w_down, layer)
    back = dest.reshape(n, 2).T.reshape(-1)
    return y_rows, back


def _experts_dense_kernel(x_ref, cw_ref, wg_ref, wu_ref, wd_ref, o_ref):
    e = pl.program_id(0)
    fi = pl.program_id(1)

    @pl.when(jnp.logical_and(e == 0, fi == 0))
    def _():
        o_ref[...] = jnp.zeros_like(o_ref)

    x = x_ref[...]
    lane = lax.broadcasted_iota(jnp.int32, (1, LANES), 1)
    cw = jnp.sum(jnp.where(lane == e, cw_ref[...], 0.0), axis=1, keepdims=True)
    hid = (_silu(_dot3(x, wg_ref[...])) * _dot3(x, wu_ref[...])) * cw
    o_ref[...] += _dot3(hid, wd_ref[...])


def _moe_dense(x, norm_w, w_router, b_router, w_gate, w_up, w_down, layer):
    n, d = x.shape
    f = w_gate.shape[-1]
    tf = f // 2
    h2, routed = _router(x, norm_w, w_router, b_router, tm=n, precise=True)
    eid = routed[:, 0:2].astype(jnp.int32)
    combine = jnp.sum(jax.nn.one_hot(eid, LANES, dtype=F32) * routed[:, 2:4, None], axis=1)
    full = lambda shape: pl.BlockSpec(shape, lambda e, fi: (0, 0))
    return pl.pallas_call(
        _experts_dense_kernel, grid=(N_EXPERTS, f // tf),
        in_specs=[full((n, d)), full((n, LANES)),
                  pl.BlockSpec((None, None, d, tf), lambda e, fi: (layer, e, 0, fi)),
                  pl.BlockSpec((None, None, d, tf), lambda e, fi: (layer, e, 0, fi)),
                  pl.BlockSpec((None, None, tf, d), lambda e, fi: (layer, e, fi, 0))],
        out_specs=full((n, d)), out_shape=jax.ShapeDtypeStruct((n, d), F32),
        compiler_params=_params("arbitrary", "arbitrary"), name="experts_dense")(
            h2, combine, w_gate, w_up, w_down)


def kernel(x_prompt, x_sample, state_hgrn, state_gdn, state_gdn_conv, cache_k, cache_v, page_table, norm_mix, w_in, hgrn_lb, hgrn_norm, gdn_conv, gdn_a_log, gdn_dt_bias, gdn_norm, sb_bias, w_branch, w_out, norm_ffn, w_router_group, b_router_group, w_router_expert, b_router_expert, w_exp_gate, w_exp_up, w_exp_down, final_norm):
    bp, tp, d = x_prompt.shape
    bs, ts, _ = x_sample.shape
    depth = w_in.shape[0]
    n_h = state_hgrn.shape[2]
    br = n_h * HEAD_DIM
    n_p, n_s = bp * tp, bs * ts
    assert n_p % IN_TILE == 0 and n_p % ROW_TILE == 0 and n_p % NORM_TILE == 0
    assert tp % CHUNK == 0 and tp % SB_TQ == 0 and ts % SUBLANES == 0 and n_s % SUBLANES == 0
    assert 2 * n_h <= LANES and br % PROJ_TN == 0
    assert n_h * ts <= LANES and ts & (ts - 1) == 0 and n_h & (n_h - 1) == 0

    heads_per_step = min(HEADS_PER_STEP, n_h)
    ab0 = 8 * br
    ab1 = ab0 + 2 * n_h
    n_main = 11 * br + 3 * d
    gate_col0 = 11 * br
    lb_all = jnp.cumsum(jax.nn.softmax(hgrn_lb.astype(F32), axis=0), axis=0)
    zeros_h = jnp.zeros((bp, n_h, HEAD_DIM, HEAD_DIM), F32)
    zeros_conv = jnp.zeros((bp, CONV_W - 1, 3 * br), F32)

    xp = x_prompt.reshape(n_p, d)
    xs = x_sample.reshape(n_s, d)
    outs_p = [[] for _ in range(5)]
    outs_s = [[] for _ in range(5)]
    hp = _rmsnorm(xp, norm_mix[0], BF16, ROW_TILE)
    hs = _rmsnorm(xs, norm_mix[0], F32, n_s)
    for l in range(depth):
        lb = (lb_all[l] - lb_all[0]).reshape(1, br)
        lb_params = (jnp.log(lb), jnp.log1p(-lb), 1.0 - lb)
        gate_params = jnp.pad(jnp.stack([gdn_a_log[l], gdn_dt_bias[l]]).astype(F32), ((0, 0), (0, LANES - n_h)))
        conv_w = gdn_conv[l].astype(F32)
        bias = sb_bias[l].astype(F32)
        w_router = jnp.pad(jnp.concatenate([w_router_group[l], w_router_expert[l]], axis=1),
                           ((0, 0), (0, LANES - N_GROUPS - N_EXPERTS)))
        b_router = jnp.pad(jnp.concatenate([b_router_group[l], b_router_expert[l]]).astype(F32),
                           (0, LANES - N_GROUPS - N_EXPERTS)).reshape(1, LANES)
        next_norm = norm_mix[l + 1] if l + 1 < depth else final_norm

        w_main = jnp.concatenate([w_in[l, :, :ab0], w_in[l, :, ab1:]], axis=1).astype(BF16)
        proj = _matmul(hp, w_main, tm=IN_TILE, tn=PROJ_TN, name="in_proj")
        proj_ab = _matmul(hp, w_in, layer=l, tm=IN_TILE, tn=LANES, col0=ab0, n_cols=LANES, name="in_proj_gates")
        o_a, hg_p = _hgrn(proj, lb_params, hgrn_norm[l], zeros_h, seq=tp, chunk=CHUNK, precise=False,
                          hp=heads_per_step)
        o_b, gd_p = _gdn(proj, proj_ab, conv_w, zeros_conv, gate_params, gdn_norm[l], zeros_h,
                         seq=tp, chunk=CHUNK, precise=False, hp=heads_per_step)
        o_c = _sb_prompt(proj, bias, bsz=bp, seq=tp, n_h=n_h, hp=min(SB_HEADS_PER_STEP, n_h))
        mixed = _merge(o_a, o_b, o_c, w_branch, l, proj, gate_col0, d, tm=ROW_TILE, precise=False)
        xp = _matmul(mixed, w_out, layer=l, tm=ROW_TILE, tn=PROJ_TN, residual=xp, name="out_proj")
        y_rows, back = _moe_sorted(xp, norm_ffn[l], w_router, b_router, w_exp_gate, w_exp_up, w_exp_down, l)
        xp, hp = _moe_add_norm(xp, y_rows, back, next_norm, BF16 if l + 1 < depth else F32)
        outs_p[0].append(hg_p)
        outs_p[1].append(gd_p)
        outs_p[2].append(proj[:, 4 * br:7 * br].reshape(bp, tp, 3 * br)[:, tp - (CONV_W - 1):])
        outs_p[3].append(proj[:, 9 * br:10 * br].reshape(bp, tp, n_h, HEAD_DIM))
        outs_p[4].append(proj[:, 10 * br:11 * br].reshape(bp, tp, n_h, HEAD_DIM))

        proj_s = jnp.concatenate([
            _matmul_precise(hs, w_in, layer=l, tn=PROJ_TN, col0=0, n_cols=ab0, name="in_proj_s0"),
            _matmul_precise(hs, w_in, layer=l, tn=PROJ_TN, col0=ab1, n_cols=n_main - ab0, name="in_proj_s1")],
            axis=1)
        proj_ab_s = _matmul_precise(hs, w_in, layer=l, tn=LANES, col0=ab0, n_cols=LANES, name="in_proj_gates_s")
        oa_s, hg_s = _hgrn(proj_s, lb_params, hgrn_norm[l], state_hgrn[l].astype(F32), seq=ts, chunk=ts,
                           precise=True, hp=heads_per_step)
        ob_s, gd_s = _gdn(proj_s, proj_ab_s, conv_w, state_gdn_conv[l].astype(F32), gate_params, gdn_norm[l],
                          state_gdn[l].astype(F32), seq=ts, chunk=ts, precise=True, hp=heads_per_step)
        oc_s = _sb_sample(proj_s, cache_k, cache_v, l, page_table, bias, bsz=bs, seq=ts, n_h=n_h)
        mixed_s = _merge(oa_s, ob_s, oc_s, w_branch, l, proj_s, gate_col0, d, tm=n_s, precise=True)
        xs = _matmul_precise(mixed_s, w_out, layer=l, tn=PROJ_TN, residual=xs, name="out_proj_s")
        y_s = _moe_dense(xs, norm_ffn[l], w_router, b_router, w_exp_gate, w_exp_up, w_exp_down, l)
        xs, hs = _add_norm(xs, y_s, next_norm, F32)
        outs_s[0].append(hg_s)
        outs_s[1].append(gd_s)
        outs_s[2].append(proj_s[:, 4 * br:7 * br].reshape(bs, ts, 3 * br)[:, ts - (CONV_W - 1):])
        outs_s[3].append(proj_s[:, 9 * br:10 * br].reshape(bs, ts, n_h, HEAD_DIM))
        outs_s[4].append(proj_s[:, 10 * br:11 * br].reshape(bs, ts, n_h, HEAD_DIM))

    y_prompt = hp.reshape(bp, tp, d)
    y_sample = hs.reshape(bs, ts, d)
    return (y_prompt, y_sample, *(jnp.stack(o) for o in outs_p), *(jnp.stack(o) for o in outs_s))
```

```python
import functools

import jax
import jax.numpy as jnp
from jax import lax
from jax.experimental import pallas as pl
from jax.experimental.pallas import tpu as pltpu

F32 = jnp.float32
BF16 = jnp.bfloat16

HEAD_DIM = 128
CONV_W = 4
CHUNK = 64
N_GROUPS = 4
EXPERTS_PER_GROUP = 8
N_EXPERTS = N_GROUPS * EXPERTS_PER_GROUP
EPS = 1e-6
LANES = 128
SUBLANES = 8
VMEM_LIMIT_BYTES = 56 * 1024 * 1024
NEG_INF = float("-inf")

IN_TILE = 1024
ROW_TILE = 512
NORM_TILE = 256
PROJ_TN = 512
MOE_TILE = 256
SB_TQ = 256
SB_TK = 128
HEADS_PER_STEP = 8
SB_PAGES_PER_STEP = 2
SB_HEADS_PER_STEP = 4


def _params(*sem):
    return pltpu.CompilerParams(dimension_semantics=sem, vmem_limit_bytes=VMEM_LIMIT_BYTES)


def _split_bf16(x, n):
    parts = []
    r = x
    for i in range(n):
        p = r.astype(BF16)
        parts.append(p)
        if i + 1 < n:
            r = r - p.astype(F32)
    return parts


_NN = (((1,), (0,)), ((), ()))
_NT = (((1,), (1,)), ((), ()))
_TN = (((0,), (0,)), ((), ()))


def _dot(a, b, dims=_NN):
    return lax.dot_general(a, b, dims, preferred_element_type=F32)


def _dotb(a, b, dims=_NN):
    return _dot(a.astype(BF16), b.astype(BF16), dims)


def _dot3(a, b, dims=_NN):
    a_hi, a_lo = _split_bf16(a, 2)
    b_hi, b_lo = _split_bf16(b, 2)
    return _dot(a_hi, b_hi, dims) + (_dot(a_hi, b_lo, dims) + _dot(a_lo, b_hi, dims))


def _mm(precise):
    return _dot3 if precise else _dotb


def _dot01_left(m01, x):
    p0, p1, p2 = _split_bf16(x, 3)
    return _dot(m01, p0) + (_dot(m01, p1) + _dot(m01, p2))


def _dot01_right(x, m01):
    p0, p1, p2 = _split_bf16(x, 3)
    return _dot(p0, m01) + (_dot(p1, m01) + _dot(p2, m01))


def _softplus(x):
    return jnp.maximum(x, 0.0) + jnp.log1p(jnp.exp(-jnp.abs(x)))


def _silu(x):
    return x * jax.nn.sigmoid(x)


def _tri(n, strict=False):
    r = lax.broadcasted_iota(jnp.int32, (n, n), 0)
    c = lax.broadcasted_iota(jnp.int32, (n, n), 1)
    return (r > c) if strict else (r >= c)


def _rms(x, w):
    return x * lax.rsqrt(jnp.mean(x * x, axis=-1, keepdims=True) + EPS) * w


def _gated_head_norm(o, w, z):
    return _rms(o, w) * _silu(z)


def _rmsnorm_kernel(x_ref, w_ref, o_ref):
    o_ref[...] = _rms(x_ref[...], w_ref[...]).astype(o_ref.dtype)


def _rmsnorm(x, w, out_dtype, tm):
    n, d = x.shape
    row = pl.BlockSpec((tm, d), lambda i: (i, 0))
    return pl.pallas_call(
        _rmsnorm_kernel, grid=(n // tm,),
        in_specs=[row, pl.BlockSpec((1, d), lambda i: (0, 0))],
        out_specs=row, out_shape=jax.ShapeDtypeStruct((n, d), out_dtype),
        compiler_params=_params("parallel"), name="rmsnorm")(x, w.reshape(1, d))


def _add_norm_kernel(x_ref, y_ref, w_ref, xo_ref, ho_ref):
    x = x_ref[...] + y_ref[...]
    xo_ref[...] = x
    ho_ref[...] = _rms(x, w_ref[...]).astype(ho_ref.dtype)


def _add_norm(x, y, w, out_dtype):
    n, d = x.shape
    return pl.pallas_call(
        _add_norm_kernel,
        out_shape=(jax.ShapeDtypeStruct((n, d), F32), jax.ShapeDtypeStruct((n, d), out_dtype)),
        compiler_params=pltpu.CompilerParams(vmem_limit_bytes=VMEM_LIMIT_BYTES),
        name="add_norm")(x, y, w.reshape(1, d))


def _row_copies(idx_ref, idx0, n_rows, src_hbm, dst_vmem, sem, start):
    def body(r, carry):
        cp = pltpu.make_async_copy(src_hbm.at[pl.ds(idx_ref[idx0 + r], 1), :], dst_vmem.at[pl.ds(r, 1), :], sem)
        if start:
            cp.start()
        else:
            cp.wait()
        return carry

    lax.fori_loop(0, n_rows, body, 0)


def _moe_add_norm_kernel(back_ref, x_ref, y_hbm, w_ref, xo_ref, ho_ref, ybuf, sems):
    i = pl.program_id(0)
    n_i = pl.num_programs(0)
    tm = x_ref.shape[0]
    n_tok = n_i * tm

    def copies(step, start):
        slot = step % 2
        for k in range(2):
            _row_copies(back_ref, k * n_tok + step * tm, tm, y_hbm, ybuf.at[slot, k], sems.at[slot], start)

    @pl.when(i == 0)
    def _():
        copies(i, True)

    @pl.when(i + 1 < n_i)
    def _():
        copies(i + 1, True)

    copies(i, False)
    slot = i % 2
    x = x_ref[...] + (ybuf[slot, 0] + ybuf[slot, 1])
    xo_ref[...] = x
    ho_ref[...] = _rms(x, w_ref[...]).astype(ho_ref.dtype)


def _moe_add_norm(x, y_rows, back, w, out_dtype):
    n, d = x.shape
    tm = NORM_TILE
    row = pl.BlockSpec((tm, d), lambda i, back: (i, 0))
    return pl.pallas_call(
        _moe_add_norm_kernel,
        grid_spec=pltpu.PrefetchScalarGridSpec(
            num_scalar_prefetch=1, grid=(n // tm,),
            in_specs=[row, pl.BlockSpec(memory_space=pl.ANY), pl.BlockSpec((1, d), lambda i, back: (0, 0))],
            out_specs=(row, row),
            scratch_shapes=[pltpu.VMEM((2, 2, tm, d), F32), pltpu.SemaphoreType.DMA((2,))]),
        out_shape=(jax.ShapeDtypeStruct((n, d), F32), jax.ShapeDtypeStruct((n, d), out_dtype)),
        compiler_params=_params("arbitrary"), name="moe_add_norm")(back, x, y_rows, w.reshape(1, d))


def _mm_kernel(a_ref, b_ref, *rest, has_res):
    o_ref = rest[-1]
    out = _dotb(a_ref[...], b_ref[...])
    if has_res:
        out = out + rest[0][...]
    o_ref[...] = out.astype(o_ref.dtype)


def _mm_wres_kernel(a_ref, b_ref, *rest, has_res):
    o_ref, w_scr = rest[-2], rest[-1]

    @pl.when(pl.program_id(1) == 0)
    def _():
        w_scr[...] = b_ref[...].astype(BF16)

    out = _dot(a_ref[...].astype(BF16), w_scr[...])
    if has_res:
        out = out + rest[0][...]
    o_ref[...] = out.astype(o_ref.dtype)


def _weight_spec(b, layer, k, tn, col_block):
    if b.ndim == 2:
        return pl.BlockSpec((k, tn), lambda *g: (0, col_block(*g)))
    return pl.BlockSpec((None, k, tn), lambda *g: (layer, 0, col_block(*g)))


def _matmul(a, b, *, tm, tn, layer=None, col0=0, n_cols=None, residual=None, weight_resident=False,
            name="matmul"):
    m, k = a.shape
    n_cols = b.shape[-1] if n_cols is None else n_cols
    blk0 = col0 // tn
    if weight_resident:
        grid, row, colb = (n_cols // tn, m // tm), (lambda j, i: i), (lambda j, i: j)
        body, scratch, sem = _mm_wres_kernel, [pltpu.VMEM((k, tn), BF16)], ("parallel", "arbitrary")
    else:
        grid, row, colb = (m // tm, n_cols // tn), (lambda i, j: i), (lambda i, j: j)
        body, scratch, sem = _mm_kernel, [], ("parallel", "parallel")
    in_specs = [pl.BlockSpec((tm, k), lambda *g: (row(*g), 0)),
                _weight_spec(b, layer, k, tn, lambda *g: blk0 + colb(*g))]
    args = [a, b]
    if residual is not None:
        in_specs.append(pl.BlockSpec((tm, tn), lambda *g: (row(*g), colb(*g))))
        args.append(residual)
    return pl.pallas_call(
        functools.partial(body, has_res=residual is not None),
        grid=grid, in_specs=in_specs,
        out_specs=pl.BlockSpec((tm, tn), lambda *g: (row(*g), colb(*g))),
        out_shape=jax.ShapeDtypeStruct((m, n_cols), F32), scratch_shapes=scratch,
        compiler_params=_params(*sem), name=name)(*args)


def _mm_precise_kernel(a_ref, b0_ref, *rest, shift, has_res):
    o_ref = rest[-1]
    a = a_ref[...]
    out = _dot3(a, b0_ref[...])
    if shift:
        out = jnp.concatenate([out, _dot3(a, rest[0][...])], axis=1)[:, shift:shift + o_ref.shape[1]]
    if has_res:
        out = out + rest[-2][...]
    o_ref[...] = out


def _matmul_precise(a, b, *, tn, layer=None, col0=0, n_cols=None, residual=None, name="matmul_precise"):
    m, k = a.shape
    n_cols = b.shape[-1] if n_cols is None else n_cols
    blk0, shift = divmod(col0, tn)
    in_specs = [pl.BlockSpec((m, k), lambda j: (0, 0)), _weight_spec(b, layer, k, tn, lambda j: blk0 + j)]
    args = [a, b]
    if shift:
        in_specs.append(_weight_spec(b, layer, k, tn, lambda j: blk0 + j + 1))
        args.append(b)
    if residual is not None:
        in_specs.append(pl.BlockSpec((m, tn), lambda j: (0, j)))
        args.append(residual)
    return pl.pallas_call(
        functools.partial(_mm_precise_kernel, shift=shift, has_res=residual is not None),
        grid=(n_cols // tn,), in_specs=in_specs,
        out_specs=pl.BlockSpec((m, tn), lambda j: (0, j)),
        out_shape=jax.ShapeDtypeStruct((m, n_cols), F32),
        compiler_params=_params("parallel"), name=name)(*args)


def _merge_kernel(oa_ref, ob_ref, oc_ref, wb_ref, ga_ref, gb_ref, gc_ref, o_ref, *scratch, precise):
    if precise:
        weight = lambda j: wb_ref[j]
    else:
        w_scr, = scratch

        @pl.when(pl.program_id(1) == 0)
        def _():
            w_scr[...] = wb_ref[...].astype(BF16)

        weight = lambda j: w_scr[j]
    acc = None
    for j, (o_r, g_r) in enumerate(((oa_ref, ga_ref), (ob_ref, gb_ref), (oc_ref, gc_ref))):
        term = jax.nn.sigmoid(g_r[...]) * _mm(precise)(o_r[...], weight(j))
        acc = term if acc is None else acc + term
    o_ref[...] = acc.astype(o_ref.dtype)


def _merge(o_a, o_b, o_c, w_branch, layer, proj, gate_col0, d_model, *, tm, precise):
    n, br = o_a.shape
    tn = PROJ_TN
    g0 = gate_col0 // tn
    gstep = d_model // tn
    o_spec = pl.BlockSpec((tm, br), lambda j, i: (i, 0))
    g_specs = [pl.BlockSpec((tm, tn), functools.partial(lambda j, i, b: (i, g0 + b * gstep + j), b=b))
               for b in range(3)]
    return pl.pallas_call(
        functools.partial(_merge_kernel, precise=precise), grid=(d_model // tn, n // tm),
        in_specs=[o_spec, o_spec, o_spec, pl.BlockSpec((None, 3, br, tn), lambda j, i: (layer, 0, 0, j))] + g_specs,
        out_specs=pl.BlockSpec((tm, tn), lambda j, i: (i, j)),
        out_shape=jax.ShapeDtypeStruct((n, d_model), F32 if precise else BF16),
        scratch_shapes=[] if precise else [pltpu.VMEM((3, br, tn), BF16)],
        compiler_params=_params("parallel", "arbitrary"), name="merge")(o_a, o_b, o_c, w_branch, proj, proj, proj)


def _hgrn_kernel(q_ref, f_ref, i_ref, g_ref, llb_ref, l1m_ref, oml_ref, nw_ref, s0_ref,
                 o_ref, sfin_ref, *scratch, chunk, precise, hp):
    c = pl.program_id(2)
    n_c = pl.num_programs(2)
    mm = _mm(precise)
    operand = (lambda x: x) if precise else (lambda x: x.astype(BF16).astype(F32))
    st_scrs, o_scrs, qkvb_scrs = scratch[:hp], scratch[hp:2 * hp], scratch[2 * hp:]
    lanes = [slice(hh * HEAD_DIM, (hh + 1) * HEAD_DIM) for hh in range(hp)]

    @pl.when(c == 0)
    def _():
        for hh in range(hp):
            st_scrs[hh][...] = s0_ref[0, hh].T

    kept = []
    for hh in range(hp):
        af = f_ref[:, lanes[hh]]
        log_sig = jnp.minimum(af, 0.0) - jnp.log1p(jnp.exp(-jnp.abs(af)))
        a = llb_ref[:, lanes[hh]]
        b2 = l1m_ref[:, lanes[hh]] + log_sig
        log_f = jnp.maximum(a, b2) + jnp.log1p(jnp.exp(-jnp.abs(a - b2)))
        k = oml_ref[:, lanes[hh]] * jax.nn.sigmoid(-af)
        q = _silu(q_ref[:, lanes[hh]])
        v = i_ref[:, lanes[hh]]
        b = _dot01_left(_tri(chunk).astype(BF16), log_f)
        st = st_scrs[hh][...]
        o_scrs[hh][...] = mm(q * jnp.exp(b), st, _NT)
        qkvb_scrs[hh][0] = q
        qkvb_scrs[hh][1] = k
        qkvb_scrs[hh][2] = operand(v)
        qkvb_scrs[hh][3] = b
        kept.append((k, v, b, st))
    for s in range(chunk):
        r0 = (s // SUBLANES) * SUBLANES
        rows = r0 + lax.broadcasted_iota(jnp.int32, (chunk - r0, 1), 0)
        for hh in range(hp):
            qkvb, o_scr = qkvb_scrs[hh], o_scrs[hh]
            d = jnp.where(rows >= s, qkvb[3, r0:, :] - qkvb[3, s:s + 1, :], NEG_INF)
            col = jnp.sum(qkvb[0, r0:, :] * qkvb[1, s:s + 1, :] * jnp.exp(d), axis=-1, keepdims=True)
            o_scr[r0:, :] += operand(col) * qkvb[2, s:s + 1, :]
    for hh in range(hp):
        k, v, b, st = kept[hh]
        o_ref[:, lanes[hh]] = _gated_head_norm(o_scrs[hh][...], nw_ref[...], g_ref[:, lanes[hh]]).astype(o_ref.dtype)
        b_end = b[chunk - 1:chunk, :]
        st_new = st * jnp.exp(b_end) + mm(v, k * jnp.exp(b_end - b), _TN)
        st_scrs[hh][...] = st_new

        @pl.when(c == n_c - 1)
        def _(hh=hh, st_new=st_new):
            sfin_ref[0, hh] = st_new.T


def _hgrn(proj, lb_params, norm_w, s0, *, seq, chunk, precise, hp):
    bsz, n_h = s0.shape[:2]
    n_c = seq // chunk
    n_hg = n_h // hp
    wide = hp * HEAD_DIM

    def col(seg):
        return pl.BlockSpec((chunk, wide), lambda b, h, c: (b * n_c + c, seg * n_hg + h))

    par = pl.BlockSpec((1, wide), lambda b, h, c: (0, h))
    state = pl.BlockSpec((1, hp, HEAD_DIM, HEAD_DIM), lambda b, h, c: (b, h, 0, 0))
    in_specs = [col(0), col(1), col(2), col(3), par, par, par,
                pl.BlockSpec((1, HEAD_DIM), lambda b, h, c: (0, 0)), state]
    args = [proj, proj, proj, proj, *lb_params, norm_w.reshape(1, HEAD_DIM), s0]
    return pl.pallas_call(
        functools.partial(_hgrn_kernel, chunk=chunk, precise=precise, hp=hp),
        grid=(bsz, n_hg, n_c), in_specs=in_specs,
        out_specs=(pl.BlockSpec((chunk, wide), lambda b, h, c: (b * n_c + c, h)), state),
        out_shape=(jax.ShapeDtypeStruct((bsz * seq, n_h * HEAD_DIM), F32 if precise else BF16),
                   jax.ShapeDtypeStruct(s0.shape, F32)),
        scratch_shapes=([pltpu.VMEM((HEAD_DIM, HEAD_DIM), F32)] * hp + [pltpu.VMEM((chunk, HEAD_DIM), F32)] * hp
                        + [pltpu.VMEM((4, chunk, HEAD_DIM), F32)] * hp),
        compiler_params=_params("parallel", "parallel", "arbitrary"), name="hgrn")(*args)


def _gdn_kernel(q_ref, k_ref, v_ref, z_ref, ab_ref, cwq_ref, cwk_ref, cwv_ref, cbq_ref, cbk_ref, cbv_ref,
                gp_ref, nw_ref, s0_ref, o_ref, sfin_ref, *scratch, chunk, precise, hp, n_h):
    hg = pl.program_id(1)
    c = pl.program_id(2)
    n_c = pl.num_programs(2)
    mm = _mm(precise)
    hist = CONV_W - 1
    base = SUBLANES - hist
    s_scrs, xp_scrs, x_scrs = scratch[:hp], scratch[hp:2 * hp], scratch[2 * hp:]
    lanes = [slice(hh * HEAD_DIM, (hh + 1) * HEAD_DIM) for hh in range(hp)]

    @pl.when(c == 0)
    def _():
        for hh in range(hp):
            s_scrs[hh][...] = s0_ref[0, hh]
            for j, cb in enumerate((cbq_ref, cbk_ref, cbv_ref)):
                xp_scrs[hh][j, base:SUBLANES, :] = cb[0, :, lanes[hh]]

    def conv(hh, j, x_r, cw_r):
        xp = xp_scrs[hh]
        xp[j, SUBLANES:SUBLANES + chunk, :] = x_r[:, lanes[hh]]
        y = xp[j, base:base + chunk, :] * cw_r[0:1, lanes[hh]]
        for t in range(1, CONV_W):
            y = y + xp[j, base + t:base + t + chunk, :] * cw_r[t:t + 1, lanes[hh]]
        xp[j, base:SUBLANES, :] = xp[j, base + chunk:SUBLANES + chunk, :]
        return _silu(y)

    ab = ab_ref[...]
    g_all = -jnp.exp(gp_ref[0:1, :]) * _softplus(ab + gp_ref[1:2, :])
    cg_all = _dot01_left(_tri(chunk).astype(BF16), g_all)
    sig_ab = jax.nn.sigmoid(ab)
    lane = lax.broadcasted_iota(jnp.int32, (1, LANES), 1)
    incl = _tri(chunk)
    strict = _tri(chunk, strict=True)
    eye = jnp.logical_and(incl, jnp.logical_not(strict))

    kept = []
    for hh in range(hp):
        h = hg * hp + hh
        qc = conv(hh, 0, q_ref, cwq_ref)
        kc = conv(hh, 1, k_ref, cwk_ref)
        vc = conv(hh, 2, v_ref, cwv_ref)
        qn = qc * lax.rsqrt(jnp.sum(qc * qc, axis=-1, keepdims=True) + EPS) * (HEAD_DIM ** -0.5)
        kn = kc * lax.rsqrt(jnp.sum(kc * kc, axis=-1, keepdims=True) + EPS)
        cg = jnp.sum(jnp.where(lane == h, cg_all, 0.0), axis=1, keepdims=True)
        beta = jnp.sum(jnp.where(lane == h + n_h, sig_ab, 0.0), axis=1, keepdims=True)
        cg_row = jnp.sum(jnp.where(eye, jnp.broadcast_to(cg, (chunk, chunk)), 0.0), axis=0, keepdims=True)
        gam = jnp.exp(jnp.where(incl, cg - cg_row, NEG_INF))
        lower = jnp.where(strict, beta * gam * mm(kn, kn, _NT), 0.0)
        s = s_scrs[hh][...]
        e_cg = jnp.exp(cg)
        x_scrs[hh][...] = beta * (vc - e_cg * mm(kn, s))
        kept.append((qn, kn, cg, gam, lower, s, e_cg))
    for j in range(chunk - 1):
        r0 = (j // SUBLANES) * SUBLANES
        for hh in range(hp):
            x_scr, lower = x_scrs[hh], kept[hh][4]
            x_scr[r0:, :] -= lower[r0:, j:j + 1] * x_scr[j:j + 1, :]
    for hh in range(hp):
        qn, kn, cg, gam, lower, s, e_cg = kept[hh]
        u = x_scrs[hh][...]
        qk = mm(qn, kn, _NT) * gam
        o = e_cg * mm(qn, s) + mm(qk, u)
        cg_end = cg[chunk - 1:chunk, :]
        s_new = jnp.exp(cg_end) * s + mm(kn * jnp.exp(cg_end - cg), u, _TN)
        s_scrs[hh][...] = s_new
        o_ref[:, lanes[hh]] = _gated_head_norm(o, nw_ref[...], z_ref[:, lanes[hh]]).astype(o_ref.dtype)

        @pl.when(c == n_c - 1)
        def _(hh=hh, s_new=s_new):
            sfin_ref[0, hh] = s_new


def _gdn(proj, proj_ab, conv_w, conv_buf, gate_params, norm_w, s0, *, seq, chunk, precise, hp):
    bsz, n_h = s0.shape[:2]
    n_c = seq // chunk
    n_hg = n_h // hp
    wide = hp * HEAD_DIM

    def col(seg):
        return pl.BlockSpec((chunk, wide), lambda b, h, c: (b * n_c + c, seg * n_hg + h))

    def cw(seg):
        return pl.BlockSpec((CONV_W, wide), lambda b, h, c: (0, seg * n_hg + h))

    def cb(seg):
        return pl.BlockSpec((1, CONV_W - 1, wide), lambda b, h, c: (b, 0, seg * n_hg + h))

    state = pl.BlockSpec((1, hp, HEAD_DIM, HEAD_DIM), lambda b, h, c: (b, h, 0, 0))
    in_specs = [col(4), col(5), col(6), col(7),
                pl.BlockSpec((chunk, LANES), lambda b, h, c: (b * n_c + c, 0)),
                cw(0), cw(1), cw(2), cb(0), cb(1), cb(2),
                pl.BlockSpec((2, LANES), lambda b, h, c: (0, 0)),
                pl.BlockSpec((1, HEAD_DIM), lambda b, h, c: (0, 0)), state]
    args = [proj, proj, proj, proj, proj_ab, conv_w, conv_w, conv_w, conv_buf, conv_buf, conv_buf,
            gate_params, norm_w.reshape(1, HEAD_DIM), s0]
    return pl.pallas_call(
        functools.partial(_gdn_kernel, chunk=chunk, precise=precise, hp=hp, n_h=n_h),
        grid=(bsz, n_hg, n_c), in_specs=in_specs,
        out_specs=(pl.BlockSpec((chunk, wide), lambda b, h, c: (b * n_c + c, h)), state),
        out_shape=(jax.ShapeDtypeStruct((bsz * seq, n_h * HEAD_DIM), F32 if precise else BF16),
                   jax.ShapeDtypeStruct(s0.shape, F32)),
        scratch_shapes=([pltpu.VMEM((HEAD_DIM, HEAD_DIM), F32)] * hp
                        + [pltpu.VMEM((3, SUBLANES + chunk, HEAD_DIM), F32)] * hp
                        + [pltpu.VMEM((chunk, HEAD_DIM), F32)] * hp),
        compiler_params=_params("parallel", "parallel", "arbitrary"), name="gdn")(*args)


def _sb_block(z, mask, run, u_incl):
    sp = _softplus(z)
    log_stay = jnp.where(mask, -sp, 0.0)
    incl = _dot01_right(log_stay, u_incl)
    later = incl - log_stay + run
    w = jnp.where(mask, jnp.exp((z - sp) + later), 0.0)
    return w, run + incl[:, 0:1]


def _sb_prompt_kernel(bias_ref, q_ref, k_ref, v_ref, o_ref, *scratch, hp):
    hg = pl.program_id(1)
    qi = pl.program_id(2)
    tq = q_ref.shape[0]
    acc_scrs, run_scrs = scratch[:hp], scratch[hp:]
    lanes = [slice(hh * HEAD_DIM, (hh + 1) * HEAD_DIM) for hh in range(hp)]
    qs = [q_ref[:, lanes[hh]].astype(BF16) for hh in range(hp)]
    biases = [bias_ref[hg * hp + hh] for hh in range(hp)]
    scale = HEAD_DIM ** -0.5
    for hh in range(hp):
        acc_scrs[hh][...] = jnp.zeros_like(acc_scrs[hh])
        run_scrs[hh][...] = jnp.zeros_like(run_scrs[hh])
    t_pos = qi * tq + lax.broadcasted_iota(jnp.int32, (tq, 1), 0)
    u_incl = _tri(SB_TK).astype(BF16)
    n_kb = (qi + 1) * (tq // SB_TK)

    def body(i, carry):
        k0 = pl.multiple_of((n_kb - 1 - i) * SB_TK, SB_TK)
        s_pos = k0 + lax.broadcasted_iota(jnp.int32, (1, SB_TK), 1)
        mask = s_pos < t_pos
        for hh in range(hp):
            kb = k_ref[pl.ds(k0, SB_TK), lanes[hh]].astype(BF16)
            vb = v_ref[pl.ds(k0, SB_TK), lanes[hh]].astype(BF16)
            z = _dot(qs[hh], kb, _NT) * scale + biases[hh]
            w, run = _sb_block(z, mask, run_scrs[hh][...], u_incl)
            acc_scrs[hh][...] += _dot(w.astype(BF16), vb)
            run_scrs[hh][...] = run
        return carry

    lax.fori_loop(0, n_kb, body, 0)
    for hh in range(hp):
        o_ref[:, lanes[hh]] = acc_scrs[hh][...].astype(o_ref.dtype)


def _sb_prompt(proj, bias, *, bsz, seq, n_h, hp, seg0):
    n_q = seq // SB_TQ
    n_hg = n_h // hp
    wide = hp * HEAD_DIM
    kv = lambda seg: pl.BlockSpec((seq, wide), lambda b, h, i, bias: (b, seg * n_hg + h))
    return pl.pallas_call(
        functools.partial(_sb_prompt_kernel, hp=hp),
        grid_spec=pltpu.PrefetchScalarGridSpec(
            num_scalar_prefetch=1, grid=(bsz, n_hg, n_q),
            in_specs=[pl.BlockSpec((SB_TQ, wide), lambda b, h, i, bias: (b * n_q + i, seg0 * n_hg + h)),
                      kv(seg0 + 1), kv(seg0 + 2)],
            out_specs=pl.BlockSpec((SB_TQ, wide), lambda b, h, i, bias: (b * n_q + i, h)),
            scratch_shapes=[pltpu.VMEM((SB_TQ, HEAD_DIM), F32)] * hp + [pltpu.VMEM((SB_TQ, 1), F32)] * hp),
        out_shape=jax.ShapeDtypeStruct((bsz * seq, n_h * HEAD_DIM), BF16),
        compiler_params=_params("parallel", "parallel", "arbitrary"), name="sb_prompt")(bias, proj, proj, proj)


def _sb_sample_kernel(pt_ref, bias_ref, q_ref, ko_ref, vo_ref, *rest, n_h, pages):
    kp_refs, vp_refs = rest[:pages], rest[pages:2 * pages]
    o_ref, acc_scr, run_scr, qall_scr, kown_scr, vown_scr = rest[2 * pages:]
    _sb_sample_body(bias_ref, q_ref, ko_ref, vo_ref, kp_refs, vp_refs, o_ref,
                    acc_scr, run_scr, qall_scr, kown_scr, vown_scr, n_h=n_h)
    del pt_ref


def _sb_sample_body(bias_ref, q_ref, ko_ref, vo_ref, kp_refs, vp_refs, o_ref,
                    acc_scr, run_scr, qall_scr, kown_scr, vown_scr, *, n_h):
    j = pl.program_id(1)
    n_j = pl.num_programs(1)
    t = q_ref.shape[0]
    n_pg = kp_refs[0].shape[0]
    t_bits = t.bit_length() - 1
    scale = HEAD_DIM ** -0.5
    lane = lax.broadcasted_iota(jnp.int32, (1, LANES), 1)
    lane_h = lax.shift_right_logical(lane, t_bits)
    lane_t = jnp.bitwise_and(lane, t - 1)
    bias_row = jnp.zeros((1, LANES), F32)
    for h in range(n_h):
        bias_row = jnp.where(lane_h == h, bias_ref[h], bias_row)

    def head_slice(h):
        return slice(h * HEAD_DIM, (h + 1) * HEAD_DIM)

    @pl.when(j == 0)
    def _():
        qall_scr[...] = jnp.zeros_like(qall_scr)
        kown_scr[...] = jnp.zeros_like(kown_scr)
        vown_scr[...] = jnp.zeros_like(vown_scr)
        for h in range(n_h):
            qall_scr[h * t:(h + 1) * t, :] = q_ref[:, head_slice(h)]
            kown_scr[h * t:(h + 1) * t, :] = ko_ref[:, head_slice(h)]
            vown_scr[h * t:(h + 1) * t, :] = vo_ref[:, head_slice(h)]
        z = _dot3(kown_scr[...], qall_scr[...], _NT) * scale + bias_row
        row = lax.broadcasted_iota(jnp.int32, (LANES, 1), 0)
        row_h = lax.shift_right_logical(row, t_bits)
        row_s = jnp.bitwise_and(row, t - 1)
        valid = jnp.logical_and(jnp.logical_and(row_h == lane_h, row_s < lane_t), lane < n_h * t)
        sp = _softplus(z)
        log_stay = jnp.where(valid, -sp, 0.0)
        later_keys =jnp.logical_and(row_h == lane_h, lane_t >= row_s)
        incl = _dot01_left(later_keys.astype(BF16), log_stay)
        w = jnp.where(valid, jnp.exp((z - sp) + (incl - log_stay)), 0.0)
        acc_scr[...] = _dot3(w.T, vown_scr[...])
        run_scr[...] = jnp.sum(log_stay, axis=0, keepdims=True)

    @pl.when(j > 0)
    def _():
        row_h = jnp.bitwise_and(lax.broadcasted_iota(jnp.int32, (n_pg * n_h, 1), 0), n_h - 1)
        valid = row_h == lane_h
        terms = []
        for kp_ref in kp_refs:
            k2 = kp_ref[...].reshape(n_pg * n_h, HEAD_DIM)
            z = _dot3(k2, qall_scr[...], _NT) * scale + bias_row
            sp = _softplus(z)
            log_stay = jnp.where(valid, -sp, 0.0)
            by_key = log_stay.reshape(n_pg, n_h, LANES)
            tail = jnp.zeros((n_h, LANES), F32)
            incl = [None] * n_pg
            for s in reversed(range(n_pg)):
                tail = tail + by_key[s]
                incl[s] = tail
            incl = jnp.stack(incl).reshape(n_pg * n_h, LANES)
            terms.append(((z - sp) + (incl - log_stay), jnp.sum(tail, axis=0, keepdims=True)))
        run = run_scr[...]
        acc = acc_scr[...]
        for (log_w, total), vp_ref in zip(terms, vp_refs):
            w = jnp.where(valid, jnp.exp(log_w + run), 0.0)
            acc = acc + _dot3(w.T, vp_ref[...].reshape(n_pg * n_h, HEAD_DIM))
            run = run + total
        acc_scr[...] = acc
        run_scr[...] = run

    @pl.when(j == n_j - 1)
    def _():
        for h in range(n_h):
            o_ref[:, head_slice(h)] = acc_scr[h * t:(h + 1) * t, :]


def _sb_sample(proj, cache_k, cache_v, layer, page_table, bias, *, bsz, seq, n_h):
    n_pages = page_table.shape[1]
    br = n_h * HEAD_DIM

    def own(seg):
        return pl.BlockSpec((seq, br), lambda b, j, pt, bias: (b, seg))

    pages = SB_PAGES_PER_STEP if n_pages % SB_PAGES_PER_STEP == 0 else 1

    def page(p):
        def index(b, j, pt, bias):
            return (layer, pt[b, n_pages - pages * (jnp.maximum(j, 1) - 1) - 1 - p], 0, 0, 0)
        return pl.BlockSpec((None, None, cache_k.shape[2], n_h, HEAD_DIM), index)

    page_specs = [page(p) for p in range(pages)]
    return pl.pallas_call(
        functools.partial(_sb_sample_kernel, n_h=n_h, pages=pages),
        grid_spec=pltpu.PrefetchScalarGridSpec(
            num_scalar_prefetch=2, grid=(bsz, n_pages // pages + 1),
            in_specs=[own(8), own(9), own(10)] + page_specs + page_specs,
            out_specs=pl.BlockSpec((seq, br), lambda b, j, pt, bias: (b, 0)),
            scratch_shapes=[pltpu.VMEM((LANES, HEAD_DIM), F32), pltpu.VMEM((1, LANES), F32)]
            + [pltpu.VMEM((LANES, HEAD_DIM), F32)] * 3),
        out_shape=jax.ShapeDtypeStruct((bsz * seq, br), F32),
        compiler_params=_params("parallel", "arbitrary"), name="sb_sample")(
            page_table, bias, proj, proj, proj, *([cache_k] * pages), *([cache_v] * pages))


def _router_kernel(x_ref, nw_ref, wr_ref, br_ref, h_ref, r_ref, *, precise):
    hn = _rms(x_ref[...], nw_ref[...])
    h_ref[...] = hn
    logits = _mm(precise)(hn, wr_ref[...]) + br_ref[...]
    lane = lax.broadcasted_iota(jnp.int32, logits.shape, 1)
    big = jnp.int32(LANES)

    def first_max(mask):
        m = jnp.max(jnp.where(mask, logits, NEG_INF), axis=-1, keepdims=True)
        idx = jnp.min(jnp.where(jnp.logical_and(mask, logits == m), lane, big), axis=-1, keepdims=True)
        return m, idx

    is_grp = lane < N_GROUPS
    g_max, grp = first_max(is_grp)
    p_grp = 1.0 / jnp.sum(jnp.where(is_grp, jnp.exp(logits - g_max), 0.0), axis=-1, keepdims=True)
    e_lo = N_GROUPS + grp * EXPERTS_PER_GROUP
    in_grp = jnp.logical_and(lane >= e_lo, lane < e_lo + EXPERTS_PER_GROUP)
    v1, i1 = first_max(in_grp)
    v2, i2 = first_max(jnp.logical_and(in_grp, lane != i1))
    e2 = jnp.exp(v2 - v1)
    w1 = (1.0 / (1.0 + e2)) * p_grp
    w2 = (e2 / (1.0 + e2)) * p_grp
    out = jnp.where(lane == 0, (i1 - N_GROUPS).astype(F32), 0.0)
    out = jnp.where(lane == 1, (i2 - N_GROUPS).astype(F32), out)
    out = jnp.where(lane == 2, w1, out)
    out = jnp.where(lane == 3, w2, out)
    r_ref[...] = out


def _router(x, norm_w, w_router, b_router, *, tm, precise):
    n, d = x.shape
    row = pl.BlockSpec((tm, d), lambda i: (i, 0))
    return pl.pallas_call(
        functools.partial(_router_kernel, precise=precise), grid=(n // tm,),
        in_specs=[row, pl.BlockSpec((1, d), lambda i: (0, 0)), pl.BlockSpec((d, LANES), lambda i: (0, 0)),
                  pl.BlockSpec((1, LANES), lambda i: (0, 0))],
        out_specs=(row, pl.BlockSpec((tm, LANES), lambda i: (i, 0))),
        out_shape=(jax.ShapeDtypeStruct((n, d), F32), jax.ShapeDtypeStruct((n, LANES), F32)),
        compiler_params=_params("parallel"), name="router")(x, norm_w.reshape(1, d), w_router, b_router)


def _new_expert(te_ref, i):
    return jnp.logical_or(i == 0, te_ref[i] != te_ref[jnp.maximum(i - 1, 0)])


def _expert_up_kernel(src_ref, te_ref, nt_ref, x_hbm, wg_ref, wu_ref, rw_ref, o_ref, xbuf, sems, wg_scr, wu_scr):
    i = pl.program_id(0)
    n_t = nt_ref[0]
    tm = o_ref.shape[0]

    def copies(step, start):
        slot = step % 2
        _row_copies(src_ref, step * tm, tm, x_hbm, xbuf.at[slot], sems.at[slot], start)

    @pl.when(i == 0)
    def _():
        copies(i, True)

    @pl.when(i + 1 < n_t)
    def _():
        copies(i + 1, True)

    @pl.when(jnp.logical_and(i < n_t, _new_expert(te_ref, i)))
    def _():
        wg_scr[...] = wg_ref[...].astype(BF16)
        wu_scr[...] = wu_ref[...].astype(BF16)

    @pl.when(i < n_t)
    def _():
        copies(i, False)
        x = xbuf[i % 2].astype(BF16)
        g = _dot(x, wg_scr[...])
        u = _dot(x, wu_scr[...])
        o_ref[...] = ((_silu(g) * u) * rw_ref[...]).astype(o_ref.dtype)

    @pl.when(i >= n_t)
    def _():
        o_ref[...] = jnp.zeros_like(o_ref)


def _expert_down_kernel(te_ref, nt_ref, h_ref, wd_ref, o_ref, wd_scr):
    i = pl.program_id(0)

    @pl.when(jnp.logical_and(i < nt_ref[0], _new_expert(te_ref, i)))
    def _():
        wd_scr[...] = wd_ref[...].astype(BF16)

    @pl.when(i < nt_ref[0])
    def _():
        o_ref[...] = _dot(h_ref[...], wd_scr[...])

    @pl.when(i >= nt_ref[0])
    def _():
        o_ref[...] = jnp.zeros_like(o_ref)


def _experts(h2, src_tok, row_w, tile_expert, n_tiles, w_gate, w_up, w_down, layer):
    r = src_tok.shape[0]
    d = h2.shape[1]
    f = w_gate.shape[-1]
    n_t = r // MOE_TILE
    hid = pl.pallas_call(
        _expert_up_kernel,
        grid_spec=pltpu.PrefetchScalarGridSpec(
            num_scalar_prefetch=3, grid=(n_t,),
            in_specs=[pl.BlockSpec(memory_space=pl.ANY),
                      pl.BlockSpec((None, None, d, f), lambda i, src, te, nt: (layer, te[i], 0, 0)),
                      pl.BlockSpec((None, None, d, f), lambda i, src, te, nt: (layer, te[i], 0, 0)),
                      pl.BlockSpec((MOE_TILE, 1), lambda i, src, te, nt: (i, 0))],
            out_specs=pl.BlockSpec((MOE_TILE, f), lambda i, src, te, nt: (i, 0)),
            scratch_shapes=[pltpu.VMEM((2, MOE_TILE, d), F32), pltpu.SemaphoreType.DMA((2,)),
                            pltpu.VMEM((d, f), BF16), pltpu.VMEM((d, f), BF16)]),
        out_shape=jax.ShapeDtypeStruct((r, f), BF16),
        compiler_params=_params("arbitrary"), name="expert_up")(
            src_tok, tile_expert, n_tiles, h2, w_gate, w_up, row_w)
    return pl.pallas_call(
        _expert_down_kernel,
        grid_spec=pltpu.PrefetchScalarGridSpec(
            num_scalar_prefetch=2, grid=(n_t,),
            in_specs=[pl.BlockSpec((MOE_TILE, f), lambda i, te, nt: (i, 0)),
                      pl.BlockSpec((None, None, f, d), lambda i, te, nt: (layer, te[i], 0, 0))],
            out_specs=pl.BlockSpec((MOE_TILE, d), lambda i, te, nt: (i, 0)),
            scratch_shapes=[pltpu.VMEM((f, d), BF16)]),
        out_shape=jax.ShapeDtypeStruct((r, d), F32),
        compiler_params=_params("arbitrary"), name="expert_down")(tile_expert, n_tiles, hid, w_down)


def _moe_schedule(eid, wsel):
    n = eid.shape[0]
    flat_e = eid.reshape(-1)
    onehot = (flat_e[:, None] == jnp.arange(N_EXPERTS, dtype=jnp.int32)[None, :]).astype(jnp.int32)
    rank = jnp.sum((jnp.cumsum(onehot, axis=0) - onehot) * onehot, axis=1)
    counts = jnp.sum(onehot, axis=0)
    tiles_e = (counts + MOE_TILE - 1) // MOE_TILE
    tiles_end = jnp.cumsum(tiles_e)
    row_start = (tiles_end - tiles_e) * MOE_TILE
    dest = row_start[flat_e] + rank
    n_tiles_max = (2 * n + MOE_TILE - 1) // MOE_TILE + N_EXPERTS
    n_rows = n_tiles_max * MOE_TILE
    src_tok = jnp.zeros((n_rows,), jnp.int32).at[dest].set(jnp.arange(2 * n, dtype=jnp.int32) // 2)
    row_w = jnp.zeros((n_rows,), F32).at[dest].set(wsel.reshape(-1))
    n_tiles = tiles_end[-1]
    tile_ids = jnp.minimum(jnp.arange(n_tiles_max, dtype=jnp.int32), n_tiles - 1)
    tile_expert = jnp.sum((tile_ids[:, None] >= tiles_end[None, :]).astype(jnp.int32), axis=1)
    return dest, src_tok, row_w.reshape(n_rows, 1), tile_expert, n_tiles.reshape(1).astype(jnp.int32)


def _moe_sorted(x, norm_w, w_router, b_router, w_gate, w_up, w_down, layer):
    n = x.shape[0]
    h2, routed = _router(x, norm_w, w_router, b_router, tm=NORM_TILE, precise=False)
    eid = routed[:, 0:2].astype(jnp.int32)
    wsel = routed[:, 2:4]
    dest, src_tok, row_w, tile_expert, n_tiles = _moe_schedule(eid, wsel)
    y_rows = _experts(h2, src_tok, row_w, tile_expert, n_tiles, w_gate, w_up, w_down, layer)
    back = dest.reshape(n, 2).T.reshape(-1)
    return y_rows, back


def _experts_dense_kernel(x_ref, cw_ref, wg_ref, wu_ref, wd_ref, o_ref):
    e = pl.program_id(0)
    fi = pl.program_id(1)

    @pl.when(jnp.logical_and(e == 0, fi == 0))
    def _():
        o_ref[...] = jnp.zeros_like(o_ref)

    x = x_ref[...]
    lane = lax.broadcasted_iota(jnp.int32, (1, LANES), 1)
    cw = jnp.sum(jnp.where(lane == e, cw_ref[...], 0.0), axis=1, keepdims=True)
    hid = (_silu(_dot3(x, wg_ref[...])) * _dot3(x, wu_ref[...])) * cw
    o_ref[...] += _dot3(hid, wd_ref[...])


def _moe_dense(x, norm_w, w_router, b_router, w_gate, w_up, w_down, layer):
    n, d = x.shape
    f = w_gate.shape[-1]
    tf = f // 2
    h2, routed = _router(x, norm_w, w_router, b_router, tm=n, precise=True)
    eid = routed[:, 0:2].astype(jnp.int32)
    combine = jnp.sum(jax.nn.one_hot(eid, LANES, dtype=F32) * routed[:, 2:4, None], axis=1)
    full = lambda shape: pl.BlockSpec(shape, lambda e, fi: (0, 0))
    return pl.pallas_call(
        _experts_dense_kernel, grid=(N_EXPERTS, f // tf),
        in_specs=[full((n, d)), full((n, LANES)),
                  pl.BlockSpec((None, None, d, tf), lambda e, fi: (layer, e, 0, fi)),
                  pl.BlockSpec((None, None, d, tf), lambda e, fi: (layer, e, 0, fi)),
                  pl.BlockSpec((None, None, tf, d), lambda e, fi: (layer, e, fi, 0))],
        out_specs=full((n, d)), out_shape=jax.ShapeDtypeStruct((n, d), F32),
        compiler_params=_params("arbitrary", "arbitrary"), name="experts_dense")(
            h2, combine, w_gate, w_up, w_down)


def kernel(x_prompt, x_sample, state_hgrn, state_gdn, state_gdn_conv, cache_k, cache_v, page_table, norm_mix, w_in, hgrn_lb, hgrn_norm, gdn_conv, gdn_a_log, gdn_dt_bias, gdn_norm, sb_bias, w_branch, w_out, norm_ffn, w_router_group, b_router_group, w_router_expert, b_router_expert, w_exp_gate, w_exp_up, w_exp_down, final_norm):
    bp, tp, d = x_prompt.shape
    bs, ts, _ = x_sample.shape
    depth = w_in.shape[0]
    n_h = state_hgrn.shape[2]
    br = n_h * HEAD_DIM
    n_p, n_s = bp * tp, bs * ts
    assert n_p % IN_TILE == 0 and n_p % ROW_TILE == 0 and n_p % NORM_TILE == 0
    assert tp % CHUNK == 0 and tp % SB_TQ == 0 and ts % SUBLANES == 0 and n_s % SUBLANES == 0
    assert 2 * n_h <= LANES and br % PROJ_TN == 0
    assert n_h * ts <= LANES and ts & (ts - 1) == 0 and n_h & (n_h - 1) == 0

    heads_per_step = min(HEADS_PER_STEP, n_h)
    ab0 = 8 * br
    ab1 = ab0 + 2 * n_h
    n_main = 11 * br + 3 * d
    gate_col0 = 11 * br
    lb_all = jnp.cumsum(jax.nn.softmax(hgrn_lb.astype(F32), axis=0), axis=0)
    zeros_h = jnp.zeros((bp, n_h, HEAD_DIM, HEAD_DIM), F32)
    zeros_conv = jnp.zeros((bp, CONV_W - 1, 3 * br), F32)

    xp = x_prompt.reshape(n_p, d)
    xs = x_sample.reshape(n_s, d)
    outs_p = [[] for _ in range(5)]
    outs_s = [[] for _ in range(5)]
    hp = _rmsnorm(xp, norm_mix[0], BF16, ROW_TILE)
    hs = _rmsnorm(xs, norm_mix[0], F32, n_s)
    for l in range(depth):
        lb = (lb_all[l] - lb_all[0]).reshape(1, br)
        lb_params = (jnp.log(lb), jnp.log1p(-lb), 1.0 - lb)
        gate_params = jnp.pad(jnp.stack([gdn_a_log[l], gdn_dt_bias[l]]).astype(F32), ((0, 0), (0, LANES - n_h)))
        conv_w = gdn_conv[l].astype(F32)
        bias = sb_bias[l].astype(F32)
        w_router = jnp.pad(jnp.concatenate([w_router_group[l], w_router_expert[l]], axis=1),
                           ((0, 0), (0, LANES - N_GROUPS - N_EXPERTS)))
        b_router = jnp.pad(jnp.concatenate([b_router_group[l], b_router_expert[l]]).astype(F32),
                           (0, LANES - N_GROUPS - N_EXPERTS)).reshape(1, LANES)
        next_norm = norm_mix[l + 1] if l + 1 < depth else final_norm

        proj_ab_cols = _matmul(hp, w_in[l, :, :ab0].astype(BF16), tm=IN_TILE, tn=PROJ_TN, name="in_proj_ab")
        proj_c_cols = _matmul(hp, w_in[l, :, ab1:].astype(BF16), tm=IN_TILE, tn=PROJ_TN, name="in_proj_c")
        proj_ab = _matmul(hp, w_in, layer=l, tm=IN_TILE, tn=LANES, col0=ab0, n_cols=LANES, name="in_proj_gates")
        o_a, hg_p = _hgrn(proj_ab_cols, lb_params, hgrn_norm[l], zeros_h, seq=tp, chunk=CHUNK, precise=False,
                          hp=heads_per_step)
        o_b, gd_p = _gdn(proj_ab_cols, proj_ab, conv_w, zeros_conv, gate_params, gdn_norm[l], zeros_h,
                         seq=tp, chunk=CHUNK, precise=False, hp=heads_per_step)
        o_c = _sb_prompt(proj_c_cols, bias, bsz=bp, seq=tp, n_h=n_h, hp=min(SB_HEADS_PER_STEP, n_h), seg0=0)
        mixed = _merge(o_a, o_b, o_c, w_branch, l, proj_c_cols, gate_col0 - ab0, d, tm=ROW_TILE, precise=False)
        xp = _matmul(mixed, w_out, layer=l, tm=ROW_TILE, tn=PROJ_TN, residual=xp, weight_resident=True,
                     name="out_proj")
        y_rows, back = _moe_sorted(xp, norm_ffn[l], w_router, b_router, w_exp_gate, w_exp_up, w_exp_down, l)
        xp, hp = _moe_add_norm(xp, y_rows, back, next_norm, BF16 if l + 1 < depth else F32)
        outs_p[0].append(hg_p)
        outs_p[1].append(gd_p)
        outs_p[2].append(proj_ab_cols[:, 4 * br:7 * br].reshape(bp, tp, 3 * br)[:, tp - (CONV_W - 1):])
        outs_p[3].append(proj_c_cols[:, br:2 * br].reshape(bp, tp, n_h, HEAD_DIM))
        outs_p[4].append(proj_c_cols[:, 2 * br:3 * br].reshape(bp, tp, n_h, HEAD_DIM))

        proj_s = jnp.concatenate([
            _matmul_precise(hs, w_in, layer=l, tn=PROJ_TN, col0=0, n_cols=ab0, name="in_proj_s0"),
            _matmul_precise(hs, w_in, layer=l, tn=PROJ_TN, col0=ab1, n_cols=n_main - ab0, name="in_proj_s1")],
            axis=1)
        proj_ab_s = _matmul_precise(hs, w_in, layer=l, tn=LANES, col0=ab0, n_cols=LANES, name="in_proj_gates_s")
        oa_s, hg_s = _hgrn(proj_s, lb_params, hgrn_norm[l], state_hgrn[l].astype(F32), seq=ts, chunk=ts,
                           precise=True, hp=heads_per_step)
        ob_s, gd_s = _gdn(proj_s, proj_ab_s, conv_w, state_gdn_conv[l].astype(F32), gate_params, gdn_norm[l],
                          state_gdn[l].astype(F32), seq=ts, chunk=ts, precise=True, hp=heads_per_step)
        oc_s = _sb_sample(proj_s, cache_k, cache_v, l, page_table, bias, bsz=bs, seq=ts, n_h=n_h)
        mixed_s = _merge(oa_s, ob_s, oc_s, w_branch, l, proj_s, gate_col0, d, tm=n_s, precise=True)
        xs = _matmul_precise(mixed_s, w_out, layer=l, tn=PROJ_TN, residual=xs, name="out_proj_s")
        y_s = _moe_dense(xs, norm_ffn[l], w_router, b_router, w_exp_gate, w_exp_up, w_exp_down, l)
        xs, hs = _add_norm(xs, y_s, next_norm, F32)
        outs_s[0].append(hg_s)
        outs_s[1].append(gd_s)
        outs_s[2].append(proj_s[:, 4 * br:7 * br].reshape(bs, ts, 3 * br)[:, ts - (CONV_W - 1):])
        outs_s[3].append(proj_s[:, 9 * br:10 * br].reshape(bs, ts, n_h, HEAD_DIM))
        outs_s[4].append(proj_s[:, 10 * br:11 * br].reshape(bs, ts, n_h, HEAD_DIM))

    y_prompt = hp.reshape(bp, tp, d)
    y_sample = hs.reshape(bs, ts, d)
    return (y_prompt, y_sample, *(jnp.stack(o) for o in outs_p), *(jnp.stack(o) for o in outs_s))
```

```python
import functools

import jax
import jax.numpy as jnp
from jax import lax
from jax.experimental import pallas as pl
from jax.experimental.pallas import tpu as pltpu

F32 = jnp.float32
BF16 = jnp.bfloat16

HEAD_DIM = 128
CONV_W = 4
CHUNK = 64
N_GROUPS = 4
EXPERTS_PER_GROUP = 8
N_EXPERTS = N_GROUPS * EXPERTS_PER_GROUP
EPS = 1e-6
LANES = 128
SUBLANES = 8
VMEM_LIMIT_BYTES = 56 * 1024 * 1024
NEG_INF = float("-inf")

IN_TILE = 1024
ROW_TILE = 512
NORM_TILE = 256
PROJ_TN = 512
MOE_TILE = 256
SB_TQ = 256
SB_TK = 128
HEADS_PER_STEP = 8
SB_PAGES_PER_STEP = 4
SB_HEADS_PER_STEP = 4


def _params(*sem):
    return pltpu.CompilerParams(dimension_semantics=sem, vmem_limit_bytes=VMEM_LIMIT_BYTES)


def _split_bf16(x, n):
    parts = []
    r = x
    for i in range(n):
        p = r.astype(BF16)
        parts.append(p)
        if i + 1 < n:
            r = r - p.astype(F32)
    return parts


_NN = (((1,), (0,)), ((), ()))
_NT = (((1,), (1,)), ((), ()))
_TN = (((0,), (0,)), ((), ()))


def _dot(a, b, dims=_NN):
    return lax.dot_general(a, b, dims, preferred_element_type=F32)


def _dotb(a, b, dims=_NN):
    return _dot(a.astype(BF16), b.astype(BF16), dims)


def _dot3(a, b, dims=_NN):
    a_hi, a_lo = _split_bf16(a, 2)
    b_hi, b_lo = _split_bf16(b, 2)
    return _dot(a_hi, b_hi, dims) + (_dot(a_hi, b_lo, dims) + _dot(a_lo, b_hi, dims))


def _mm(precise):
    return _dot3 if precise else _dotb


def _dot01_left(m01, x):
    p0, p1, p2 = _split_bf16(x, 3)
    return _dot(m01, p0) + (_dot(m01, p1) + _dot(m01, p2))


def _dot01_right(x, m01):
    p0, p1, p2 = _split_bf16(x, 3)
    return _dot(p0, m01) + (_dot(p1, m01) + _dot(p2, m01))


def _softplus(x):
    return jnp.maximum(x, 0.0) + jnp.log1p(jnp.exp(-jnp.abs(x)))


def _silu(x):
    return x * jax.nn.sigmoid(x)


def _tri(n, strict=False):
    r = lax.broadcasted_iota(jnp.int32, (n, n), 0)
    c = lax.broadcasted_iota(jnp.int32, (n, n), 1)
    return (r > c) if strict else (r >= c)


def _rms(x, w):
    return x * lax.rsqrt(jnp.mean(x * x, axis=-1, keepdims=True) + EPS) * w


def _gated_head_norm(o, w, z):
    return _rms(o, w) * _silu(z)


def _rmsnorm_kernel(x_ref, w_ref, o_ref):
    o_ref[...] = _rms(x_ref[...], w_ref[...]).astype(o_ref.dtype)


def _rmsnorm(x, w, out_dtype, tm):
    n, d = x.shape
    row = pl.BlockSpec((tm, d), lambda i: (i, 0))
    return pl.pallas_call(
        _rmsnorm_kernel, grid=(n // tm,),
        in_specs=[row, pl.BlockSpec((1, d), lambda i: (0, 0))],
        out_specs=row, out_shape=jax.ShapeDtypeStruct((n, d), out_dtype),
        compiler_params=_params("parallel"), name="rmsnorm")(x, w.reshape(1, d))


def _add_norm_kernel(x_ref, y_ref, w_ref, xo_ref, ho_ref):
    x = x_ref[...] + y_ref[...]
    xo_ref[...] = x
    ho_ref[...] = _rms(x, w_ref[...]).astype(ho_ref.dtype)


def _add_norm(x, y, w, out_dtype):
    n, d = x.shape
    return pl.pallas_call(
        _add_norm_kernel,
        out_shape=(jax.ShapeDtypeStruct((n, d), F32), jax.ShapeDtypeStruct((n, d), out_dtype)),
        compiler_params=pltpu.CompilerParams(vmem_limit_bytes=VMEM_LIMIT_BYTES),
        name="add_norm")(x, y, w.reshape(1, d))


def _row_copies(idx_ref, idx0, n_rows, src_hbm, dst_vmem, sem, start):
    def body(r, carry):
        cp = pltpu.make_async_copy(src_hbm.at[pl.ds(idx_ref[idx0 + r], 1), :], dst_vmem.at[pl.ds(r, 1), :], sem)
        if start:
            cp.start()
        else:
            cp.wait()
        return carry

    lax.fori_loop(0, n_rows, body, 0)


def _moe_add_norm_kernel(back_ref, x_ref, y_hbm, w_ref, xo_ref, ho_ref, ybuf, sems):
    i = pl.program_id(0)
    n_i = pl.num_programs(0)
    tm = x_ref.shape[0]
    n_tok = n_i * tm

    def copies(step, start):
        slot = step % 2
        for k in range(2):
            _row_copies(back_ref, k * n_tok + step * tm, tm, y_hbm, ybuf.at[slot, k], sems.at[slot], start)

    @pl.when(i == 0)
    def _():
        copies(i, True)

    @pl.when(i + 1 < n_i)
    def _():
        copies(i + 1, True)

    copies(i, False)
    slot = i % 2
    x = x_ref[...] + (ybuf[slot, 0] + ybuf[slot, 1])
    xo_ref[...] = x
    ho_ref[...] = _rms(x, w_ref[...]).astype(ho_ref.dtype)


def _moe_add_norm(x, y_rows, back, w, out_dtype):
    n, d = x.shape
    tm = NORM_TILE
    row = pl.BlockSpec((tm, d), lambda i, back: (i, 0))
    return pl.pallas_call(
        _moe_add_norm_kernel,
        grid_spec=pltpu.PrefetchScalarGridSpec(
            num_scalar_prefetch=1, grid=(n // tm,),
            in_specs=[row, pl.BlockSpec(memory_space=pl.ANY), pl.BlockSpec((1, d), lambda i, back: (0, 0))],
            out_specs=(row, row),
            scratch_shapes=[pltpu.VMEM((2, 2, tm, d), F32), pltpu.SemaphoreType.DMA((2,))]),
        out_shape=(jax.ShapeDtypeStruct((n, d), F32), jax.ShapeDtypeStruct((n, d), out_dtype)),
        compiler_params=_params("arbitrary"), name="moe_add_norm")(back, x, y_rows, w.reshape(1, d))


def _mm_kernel(a_ref, b_ref, *rest, has_res):
    o_ref = rest[-1]
    out = _dotb(a_ref[...], b_ref[...])
    if has_res:
        out = out + rest[0][...]
    o_ref[...] = out.astype(o_ref.dtype)


def _mm_wres_kernel(a_ref, b_ref, *rest, has_res):
    o_ref, w_scr = rest[-2], rest[-1]

    @pl.when(pl.program_id(1) == 0)
    def _():
        w_scr[...] = b_ref[...].astype(BF16)

    out = _dot(a_ref[...].astype(BF16), w_scr[...])
    if has_res:
        out = out + rest[0][...]
    o_ref[...] = out.astype(o_ref.dtype)


def _weight_spec(b, layer, k, tn, col_block):
    if b.ndim == 2:
        return pl.BlockSpec((k, tn), lambda *g: (0, col_block(*g)))
    return pl.BlockSpec((None, k, tn), lambda *g: (layer, 0, col_block(*g)))


def _matmul(a, b, *, tm, tn, layer=None, col0=0, n_cols=None, residual=None, weight_resident=False,
            name="matmul"):
    m, k = a.shape
    n_cols = b.shape[-1] if n_cols is None else n_cols
    blk0 = col0 // tn
    if weight_resident:
        grid, row, colb = (n_cols // tn, m // tm), (lambda j, i: i), (lambda j, i: j)
        body, scratch, sem = _mm_wres_kernel, [pltpu.VMEM((k, tn), BF16)], ("parallel", "arbitrary")
    else:
        grid, row, colb = (m // tm, n_cols // tn), (lambda i, j: i), (lambda i, j: j)
        body, scratch, sem = _mm_kernel, [], ("parallel", "parallel")
    in_specs = [pl.BlockSpec((tm, k), lambda *g: (row(*g), 0)),
                _weight_spec(b, layer, k, tn, lambda *g: blk0 + colb(*g))]
    args = [a, b]
    if residual is not None:
        in_specs.append(pl.BlockSpec((tm, tn), lambda *g: (row(*g), colb(*g))))
        args.append(residual)
    return pl.pallas_call(
        functools.partial(body, has_res=residual is not None),
        grid=grid, in_specs=in_specs,
        out_specs=pl.BlockSpec((tm, tn), lambda *g: (row(*g), colb(*g))),
        out_shape=jax.ShapeDtypeStruct((m, n_cols), F32), scratch_shapes=scratch,
        compiler_params=_params(*sem), name=name)(*args)


def _mm_precise_kernel(a_ref, b_ref, *rest, presplit, has_res):
    o_ref = rest[-1]
    if presplit:
        a_hi, a_lo = _split_bf16(a_ref[...], 2)
        b_hi, b_lo = b_ref[...], rest[0][...]
        out = _dot(a_hi, b_hi) + (_dot(a_hi, b_lo) + _dot(a_lo, b_hi))
    else:
        out = _dot3(a_ref[...], b_ref[...])
    if has_res:
        out = out + rest[-2][...]
    o_ref[...] = out


def _matmul_precise(a, b, *, tn, b_lo=None, layer=None, residual=None, name="matmul_precise"):
    m, k = a.shape
    n_cols = b.shape[-1]
    in_specs = [pl.BlockSpec((m, k), lambda j: (0, 0)), _weight_spec(b, layer, k, tn, lambda j: j)]
    args = [a, b]
    if b_lo is not None:
        in_specs.append(_weight_spec(b_lo, layer, k, tn, lambda j: j))
        args.append(b_lo)
    if residual is not None:
        in_specs.append(pl.BlockSpec((m, tn), lambda j: (0, j)))
        args.append(residual)
    return pl.pallas_call(
        functools.partial(_mm_precise_kernel, presplit=b_lo is not None, has_res=residual is not None),
        grid=(n_cols // tn,), in_specs=in_specs,
        out_specs=pl.BlockSpec((m, tn), lambda j: (0, j)),
        out_shape=jax.ShapeDtypeStruct((m, n_cols), F32),
        compiler_params=_params("parallel"), name=name)(*args)


def _merge_kernel(oa_ref, ob_ref, oc_ref, wb_ref, ga_ref, gb_ref, gc_ref, o_ref, *scratch, precise):
    if precise:
        weight = lambda j: wb_ref[j]
    else:
        w_scr, = scratch

        @pl.when(pl.program_id(1) == 0)
        def _():
            w_scr[...] = wb_ref[...].astype(BF16)

        weight = lambda j: w_scr[j]
    acc = None
    for j, (o_r, g_r) in enumerate(((oa_ref, ga_ref), (ob_ref, gb_ref), (oc_ref, gc_ref))):
        term = jax.nn.sigmoid(g_r[...]) * _mm(precise)(o_r[...], weight(j))
        acc = term if acc is None else acc + term
    o_ref[...] = acc.astype(o_ref.dtype)


def _merge(o_a, o_b, o_c, w_branch, layer, proj, gate_col0, d_model, *, tm, precise):
    n, br = o_a.shape
    tn = PROJ_TN
    g0 = gate_col0 // tn
    gstep = d_model // tn
    o_spec = pl.BlockSpec((tm, br), lambda j, i: (i, 0))
    g_specs = [pl.BlockSpec((tm, tn), functools.partial(lambda j, i, b: (i, g0 + b * gstep + j), b=b))
               for b in range(3)]
    return pl.pallas_call(
        functools.partial(_merge_kernel, precise=precise), grid=(d_model // tn, n // tm),
        in_specs=[o_spec, o_spec, o_spec, pl.BlockSpec((None, 3, br, tn), lambda j, i: (layer, 0, 0, j))] + g_specs,
        out_specs=pl.BlockSpec((tm, tn), lambda j, i: (i, j)),
        out_shape=jax.ShapeDtypeStruct((n, d_model), F32 if precise else BF16),
        scratch_shapes=[] if precise else [pltpu.VMEM((3, br, tn), BF16)],
        compiler_params=_params("parallel", "arbitrary"), name="merge")(o_a, o_b, o_c, w_branch, proj, proj, proj)


def _hgrn_kernel(q_ref, f_ref, i_ref, g_ref, llb_ref, l1m_ref, oml_ref, nw_ref, s0_ref,
                 o_ref, sfin_ref, *scratch, chunk, precise, hp):
    c = pl.program_id(2)
    n_c = pl.num_programs(2)
    mm = _mm(precise)
    operand = (lambda x: x) if precise else (lambda x: x.astype(BF16).astype(F32))
    st_scrs, o_scrs, qkvb_scrs = scratch[:hp], scratch[hp:2 * hp], scratch[2 * hp:]
    lanes = [slice(hh * HEAD_DIM, (hh + 1) * HEAD_DIM) for hh in range(hp)]

    @pl.when(c == 0)
    def _():
        for hh in range(hp):
            st_scrs[hh][...] = s0_ref[0, hh].T

    kept = []
    for hh in range(hp):
        af = f_ref[:, lanes[hh]]
        log_sig = jnp.minimum(af, 0.0) - jnp.log1p(jnp.exp(-jnp.abs(af)))
        a = llb_ref[:, lanes[hh]]
        b2 = l1m_ref[:, lanes[hh]] + log_sig
        log_f = jnp.maximum(a, b2) + jnp.log1p(jnp.exp(-jnp.abs(a - b2)))
        k = oml_ref[:, lanes[hh]] * jax.nn.sigmoid(-af)
        q = _silu(q_ref[:, lanes[hh]])
        v = i_ref[:, lanes[hh]]
        b = _dot01_left(_tri(chunk).astype(BF16), log_f)
        st = st_scrs[hh][...]
        o_scrs[hh][...] = mm(q * jnp.exp(b), st, _NT)
        qkvb_scrs[hh][0] = q
        qkvb_scrs[hh][1] = k
        qkvb_scrs[hh][2] = operand(v)
        qkvb_scrs[hh][3] = b
        kept.append((k, v, b, st))
    for s in range(chunk):
        r0 = (s // SUBLANES) * SUBLANES
        rows = r0 + lax.broadcasted_iota(jnp.int32, (chunk - r0, 1), 0)
        for hh in range(hp):
            qkvb, o_scr = qkvb_scrs[hh], o_scrs[hh]
            d = jnp.where(rows >= s, qkvb[3, r0:, :] - qkvb[3, s:s + 1, :], NEG_INF)
            col = jnp.sum(qkvb[0, r0:, :] * qkvb[1, s:s + 1, :] * jnp.exp(d), axis=-1, keepdims=True)
            o_scr[r0:, :] += operand(col) * qkvb[2, s:s + 1, :]
    for hh in range(hp):
        k, v, b, st = kept[hh]
        o_ref[:, lanes[hh]] = _gated_head_norm(o_scrs[hh][...], nw_ref[...], g_ref[:, lanes[hh]]).astype(o_ref.dtype)
        b_end = b[chunk - 1:chunk, :]
        st_new = st * jnp.exp(b_end) + mm(v, k * jnp.exp(b_end - b), _TN)
        st_scrs[hh][...] = st_new

        @pl.when(c == n_c - 1)
        def _(hh=hh, st_new=st_new):
            sfin_ref[0, hh] = st_new.T


def _hgrn(proj, lb_params, norm_w, s0, *, seq, chunk, precise, hp):
    bsz, n_h = s0.shape[:2]
    n_c = seq // chunk
    n_hg = n_h // hp
    wide = hp * HEAD_DIM

    def col(seg):
        return pl.BlockSpec((chunk, wide), lambda b, h, c: (b * n_c + c, seg * n_hg + h))

    par = pl.BlockSpec((1, wide), lambda b, h, c: (0, h))
    state = pl.BlockSpec((1, hp, HEAD_DIM, HEAD_DIM), lambda b, h, c: (b, h, 0, 0))
    in_specs = [col(0), col(1), col(2), col(3), par, par, par,
                pl.BlockSpec((1, HEAD_DIM), lambda b, h, c: (0, 0)), state]
    args = [proj, proj, proj, proj, *lb_params, norm_w.reshape(1, HEAD_DIM), s0]
    return pl.pallas_call(
        functools.partial(_hgrn_kernel, chunk=chunk, precise=precise, hp=hp),
        grid=(bsz, n_hg, n_c), in_specs=in_specs,
        out_specs=(pl.BlockSpec((chunk, wide), lambda b, h, c: (b * n_c + c, h)), state),
        out_shape=(jax.ShapeDtypeStruct((bsz * seq, n_h * HEAD_DIM), F32 if precise else BF16),
                   jax.ShapeDtypeStruct(s0.shape, F32)),
        scratch_shapes=([pltpu.VMEM((HEAD_DIM, HEAD_DIM), F32)] * hp + [pltpu.VMEM((chunk, HEAD_DIM), F32)] * hp
                        + [pltpu.VMEM((4, chunk, HEAD_DIM), F32)] * hp),
        compiler_params=_params("parallel", "parallel", "arbitrary"), name="hgrn")(*args)


def _gdn_kernel(q_ref, k_ref, v_ref, z_ref, ab_ref, cwq_ref, cwk_ref, cwv_ref, cbq_ref, cbk_ref, cbv_ref,
                gp_ref, nw_ref, s0_ref, o_ref, sfin_ref, *scratch, chunk, precise, hp, n_h):
    hg = pl.program_id(1)
    c = pl.program_id(2)
    n_c = pl.num_programs(2)
    mm = _mm(precise)
    hist = CONV_W - 1
    base = SUBLANES - hist
    s_scrs, xp_scrs, x_scrs = scratch[:hp], scratch[hp:2 * hp], scratch[2 * hp:]
    lanes = [slice(hh * HEAD_DIM, (hh + 1) * HEAD_DIM) for hh in range(hp)]

    @pl.when(c == 0)
    def _():
        for hh in range(hp):
            s_scrs[hh][...] = s0_ref[0, hh]
            for j, cb in enumerate((cbq_ref, cbk_ref, cbv_ref)):
                xp_scrs[hh][j, base:SUBLANES, :] = cb[0, :, lanes[hh]]

    def conv(hh, j, x_r, cw_r):
        xp = xp_scrs[hh]
        xp[j, SUBLANES:SUBLANES + chunk, :] = x_r[:, lanes[hh]]
        y = xp[j, base:base + chunk, :] * cw_r[0:1, lanes[hh]]
        for t in range(1, CONV_W):
            y = y + xp[j, base + t:base + t + chunk, :] * cw_r[t:t + 1, lanes[hh]]
        xp[j, base:SUBLANES, :] = xp[j, base + chunk:SUBLANES + chunk, :]
        return _silu(y)

    ab = ab_ref[...]
    g_all = -jnp.exp(gp_ref[0:1, :]) * _softplus(ab + gp_ref[1:2, :])
    cg_all = _dot01_left(_tri(chunk).astype(BF16), g_all)
    sig_ab = jax.nn.sigmoid(ab)
    lane = lax.broadcasted_iota(jnp.int32, (1, LANES), 1)
    incl = _tri(chunk)
    strict = _tri(chunk, strict=True)
    eye = jnp.logical_and(incl, jnp.logical_not(strict))

    kept = []
    for hh in range(hp):
        h = hg * hp + hh
        qc = conv(hh, 0, q_ref, cwq_ref)
        kc = conv(hh, 1, k_ref, cwk_ref)
        vc = conv(hh, 2, v_ref, cwv_ref)
        qn = qc * lax.rsqrt(jnp.sum(qc * qc, axis=-1, keepdims=True) + EPS) * (HEAD_DIM ** -0.5)
        kn = kc * lax.rsqrt(jnp.sum(kc * kc, axis=-1, keepdims=True) + EPS)
        cg = jnp.sum(jnp.where(lane == h, cg_all, 0.0), axis=1, keepdims=True)
        beta = jnp.sum(jnp.where(lane == h + n_h, sig_ab, 0.0), axis=1, keepdims=True)
        cg_row = jnp.sum(jnp.where(eye, jnp.broadcast_to(cg, (chunk, chunk)), 0.0), axis=0, keepdims=True)
        gam = jnp.exp(jnp.where(incl, cg - cg_row, NEG_INF))
        lower = jnp.where(strict, beta * gam * mm(kn, kn, _NT), 0.0)
        s = s_scrs[hh][...]
        e_cg = jnp.exp(cg)
        x_scrs[hh][...] = beta * (vc - e_cg * mm(kn, s))
        kept.append((qn, kn, cg, gam, lower, s, e_cg))
    for j in range(chunk - 1):
        r0 = (j // SUBLANES) * SUBLANES
        for hh in range(hp):
            x_scr, lower = x_scrs[hh], kept[hh][4]
            x_scr[r0:, :] -= lower[r0:, j:j + 1] * x_scr[j:j + 1, :]
    for hh in range(hp):
        qn, kn, cg, gam, lower, s, e_cg = kept[hh]
        u = x_scrs[hh][...]
        qk = mm(qn, kn, _NT) * gam
        o = e_cg * mm(qn, s) + mm(qk, u)
        cg_end = cg[chunk - 1:chunk, :]
        s_new = jnp.exp(cg_end) * s + mm(kn * jnp.exp(cg_end - cg), u, _TN)
        s_scrs[hh][...] = s_new
        o_ref[:, lanes[hh]] = _gated_head_norm(o, nw_ref[...], z_ref[:, lanes[hh]]).astype(o_ref.dtype)

        @pl.when(c == n_c - 1)
        def _(hh=hh, s_new=s_new):
            sfin_ref[0, hh] = s_new


def _gdn(proj, proj_ab, conv_w, conv_buf, gate_params, norm_w, s0, *, seq, chunk, precise, hp):
    bsz, n_h = s0.shape[:2]
    n_c = seq // chunk
    n_hg = n_h // hp
    wide = hp * HEAD_DIM

    def col(seg):
        return pl.BlockSpec((chunk, wide), lambda b, h, c: (b * n_c + c, seg * n_hg + h))

    def cw(seg):
        return pl.BlockSpec((CONV_W, wide), lambda b, h, c: (0, seg * n_hg + h))

    def cb(seg):
        return pl.BlockSpec((1, CONV_W - 1, wide), lambda b, h, c: (b, 0, seg * n_hg + h))

    state = pl.BlockSpec((1, hp, HEAD_DIM, HEAD_DIM), lambda b, h, c: (b, h, 0, 0))
    in_specs = [col(4), col(5), col(6), col(7),
                pl.BlockSpec((chunk, LANES), lambda b, h, c: (b * n_c + c, 0)),
                cw(0), cw(1), cw(2), cb(0), cb(1), cb(2),
                pl.BlockSpec((2, LANES), lambda b, h, c: (0, 0)),
                pl.BlockSpec((1, HEAD_DIM), lambda b, h, c: (0, 0)), state]
    args = [proj, proj, proj, proj, proj_ab, conv_w, conv_w, conv_w, conv_buf, conv_buf, conv_buf,
            gate_params, norm_w.reshape(1, HEAD_DIM), s0]
    return pl.pallas_call(
        functools.partial(_gdn_kernel, chunk=chunk, precise=precise, hp=hp, n_h=n_h),
        grid=(bsz, n_hg, n_c), in_specs=in_specs,
        out_specs=(pl.BlockSpec((chunk, wide), lambda b, h, c: (b * n_c + c, h)), state),
        out_shape=(jax.ShapeDtypeStruct((bsz * seq, n_h * HEAD_DIM), F32 if precise else BF16),
                   jax.ShapeDtypeStruct(s0.shape, F32)),
        scratch_shapes=([pltpu.VMEM((HEAD_DIM, HEAD_DIM), F32)] * hp
                        + [pltpu.VMEM((3, SUBLANES + chunk, HEAD_DIM), F32)] * hp
                        + [pltpu.VMEM((chunk, HEAD_DIM), F32)] * hp),
        compiler_params=_params("parallel", "parallel", "arbitrary"), name="gdn")(*args)


def _sb_block(z, mask, run, u_incl):
    sp = _softplus(z)
    log_stay = jnp.where(mask, -sp, 0.0)
    incl = _dot01_right(log_stay, u_incl)
    later = incl - log_stay + run
    w = jnp.where(mask, jnp.exp((z - sp) + later), 0.0)
    return w, run + incl[:, 0:1]


def _sb_prompt_kernel(bias_ref, q_ref, k_ref, v_ref, o_ref, *scratch, hp):
    hg = pl.program_id(1)
    qi = pl.program_id(2)
    tq = q_ref.shape[0]
    acc_scrs, run_scrs = scratch[:hp], scratch[hp:]
    lanes = [slice(hh * HEAD_DIM, (hh + 1) * HEAD_DIM) for hh in range(hp)]
    qs = [q_ref[:, lanes[hh]].astype(BF16) for hh in range(hp)]
    biases = [bias_ref[hg * hp + hh] for hh in range(hp)]
    scale = HEAD_DIM ** -0.5
    for hh in range(hp):
        acc_scrs[hh][...] = jnp.zeros_like(acc_scrs[hh])
        run_scrs[hh][...] = jnp.zeros_like(run_scrs[hh])
    t_pos = qi * tq + lax.broadcasted_iota(jnp.int32, (tq, 1), 0)
    u_incl = _tri(SB_TK).astype(BF16)
    n_kb = (qi + 1) * (tq // SB_TK)

    def body(i, carry):
        k0 = pl.multiple_of((n_kb - 1 - i) * SB_TK, SB_TK)
        s_pos = k0 + lax.broadcasted_iota(jnp.int32, (1, SB_TK), 1)
        mask = s_pos < t_pos
        for hh in range(hp):
            kb = k_ref[pl.ds(k0, SB_TK), lanes[hh]].astype(BF16)
            vb = v_ref[pl.ds(k0, SB_TK), lanes[hh]].astype(BF16)
            z = _dot(qs[hh], kb, _NT) * scale + biases[hh]
            w, run = _sb_block(z, mask, run_scrs[hh][...], u_incl)
            acc_scrs[hh][...] += _dot(w.astype(BF16), vb)
            run_scrs[hh][...] = run
        return carry

    lax.fori_loop(0, n_kb, body, 0)
    for hh in range(hp):
        o_ref[:, lanes[hh]] = acc_scrs[hh][...].astype(o_ref.dtype)


def _sb_prompt(proj, bias, *, bsz, seq, n_h, hp, seg0):
    n_q = seq // SB_TQ
    n_hg = n_h // hp
    wide = hp * HEAD_DIM
    kv = lambda seg: pl.BlockSpec((seq, wide), lambda b, h, i, bias: (b, seg * n_hg + h))
    return pl.pallas_call(
        functools.partial(_sb_prompt_kernel, hp=hp),
        grid_spec=pltpu.PrefetchScalarGridSpec(
            num_scalar_prefetch=1, grid=(bsz, n_hg, n_q),
            in_specs=[pl.BlockSpec((SB_TQ, wide), lambda b, h, i, bias: (b * n_q + i, seg0 * n_hg + h)),
                      kv(seg0 + 1), kv(seg0 + 2)],
            out_specs=pl.BlockSpec((SB_TQ, wide), lambda b, h, i, bias: (b * n_q + i, h)),
            scratch_shapes=[pltpu.VMEM((SB_TQ, HEAD_DIM), F32)] * hp + [pltpu.VMEM((SB_TQ, 1), F32)] * hp),
        out_shape=jax.ShapeDtypeStruct((bsz * seq, n_h * HEAD_DIM), BF16),
        compiler_params=_params("parallel", "parallel", "arbitrary"), name="sb_prompt")(bias, proj, proj, proj)


def _sb_sample_kernel(pt_ref, bias_ref, q_ref, ko_ref, vo_ref, *rest, n_h, pages):
    kp_refs, vp_refs = rest[:pages], rest[pages:2 * pages]
    o_ref, acc_scr, run_scr, qall_scr, kown_scr, vown_scr = rest[2 * pages:]
    _sb_sample_body(bias_ref, q_ref, ko_ref, vo_ref, kp_refs, vp_refs, o_ref,
                    acc_scr, run_scr, qall_scr, kown_scr, vown_scr, n_h=n_h)
    del pt_ref


def _sb_sample_body(bias_ref, q_ref, ko_ref, vo_ref, kp_refs, vp_refs, o_ref,
                    acc_scr, run_scr, qall_scr, kown_scr, vown_scr, *, n_h):
    j = pl.program_id(1)
    n_j = pl.num_programs(1)
    t = q_ref.shape[0]
    n_pg = kp_refs[0].shape[0]
    t_bits = t.bit_length() - 1
    scale = HEAD_DIM ** -0.5
    lane = lax.broadcasted_iota(jnp.int32, (1, LANES), 1)
    lane_h = lax.shift_right_logical(lane, t_bits)
    lane_t = jnp.bitwise_and(lane, t - 1)
    bias_row = jnp.zeros((1, LANES), F32)
    for h in range(n_h):
        bias_row = jnp.where(lane_h == h, bias_ref[h], bias_row)

    def head_slice(h):
        return slice(h * HEAD_DIM, (h + 1) * HEAD_DIM)

    @pl.when(j == 0)
    def _():
        qall_scr[...] = jnp.zeros_like(qall_scr)
        kown_scr[...] = jnp.zeros_like(kown_scr)
        vown_scr[...] = jnp.zeros_like(vown_scr)
        for h in range(n_h):
            qall_scr[h * t:(h + 1) * t, :] = q_ref[:, head_slice(h)]
            kown_scr[h * t:(h + 1) * t, :] = ko_ref[:, head_slice(h)]
            vown_scr[h * t:(h + 1) * t, :] = vo_ref[:, head_slice(h)]
        z = _dot3(kown_scr[...], qall_scr[...], _NT) * scale + bias_row
        row = lax.broadcasted_iota(jnp.int32, (LANES, 1), 0)
        row_h = lax.shift_right_logical(row, t_bits)
        row_s = jnp.bitwise_and(row, t - 1)
        valid = jnp.logical_and(jnp.logical_and(row_h == lane_h, row_s < lane_t), lane < n_h * t)
        sp = _softplus(z)
        log_stay = jnp.where(valid, -sp, 0.0)
        later_keys =jnp.logical_and(row_h == lane_h, lane_t >= row_s)
        incl = _dot01_left(later_keys.astype(BF16), log_stay)
        w = jnp.where(valid, jnp.exp((z - sp) + (incl - log_stay)), 0.0)
        acc_scr[...] = _dot3(w.T, vown_scr[...])
        run_scr[...] = jnp.sum(log_stay, axis=0, keepdims=True)

    @pl.when(j > 0)
    def _():
        row_h = jnp.bitwise_and(lax.broadcasted_iota(jnp.int32, (n_pg * n_h, 1), 0), n_h - 1)
        valid = row_h == lane_h
        terms = []
        for kp_ref in kp_refs:
            k2 = kp_ref[...].reshape(n_pg * n_h, HEAD_DIM)
            z = _dot3(k2, qall_scr[...], _NT) * scale + bias_row
            sp = _softplus(z)
            log_stay = jnp.where(valid, -sp, 0.0)
            by_key = log_stay.reshape(n_pg, n_h, LANES)
            tail = jnp.zeros((n_h, LANES), F32)
            incl = [None] * n_pg
            for s in reversed(range(n_pg)):
                tail = tail + by_key[s]
                incl[s] = tail
            incl = jnp.stack(incl).reshape(n_pg * n_h, LANES)
            terms.append(((z - sp) + (incl - log_stay), jnp.sum(tail, axis=0, keepdims=True)))
        run = run_scr[...]
        acc = acc_scr[...]
        for (log_w, total), vp_ref in zip(terms, vp_refs):
            w = jnp.where(valid, jnp.exp(log_w + run), 0.0)
            acc = acc + _dot3(w.T, vp_ref[...].reshape(n_pg * n_h, HEAD_DIM))
            run = run + total
        acc_scr[...] = acc
        run_scr[...] = run

    @pl.when(j == n_j - 1)
    def _():
        for h in range(n_h):
            o_ref[:, head_slice(h)] = acc_scr[h * t:(h + 1) * t, :]


def _sb_sample(proj, cache_k, cache_v, layer, page_table, bias, *, bsz, seq, n_h):
    n_pages = page_table.shape[1]
    br = n_h * HEAD_DIM

    def own(seg):
        return pl.BlockSpec((seq, br), lambda b, j, pt, bias: (b, seg))

    pages = SB_PAGES_PER_STEP if n_pages % SB_PAGES_PER_STEP == 0 else 1

    def page(p):
        def index(b, j, pt, bias):
            return (layer, pt[b, n_pages - pages * (jnp.maximum(j, 1) - 1) - 1 - p], 0, 0, 0)
        return pl.BlockSpec((None, None, cache_k.shape[2], n_h, HEAD_DIM), index)

    page_specs = [page(p) for p in range(pages)]
    return pl.pallas_call(
        functools.partial(_sb_sample_kernel, n_h=n_h, pages=pages),
        grid_spec=pltpu.PrefetchScalarGridSpec(
            num_scalar_prefetch=2, grid=(bsz, n_pages // pages + 1),
            in_specs=[own(8), own(9), own(10)] + page_specs + page_specs,
            out_specs=pl.BlockSpec((seq, br), lambda b, j, pt, bias: (b, 0)),
            scratch_shapes=[pltpu.VMEM((LANES, HEAD_DIM), F32), pltpu.VMEM((1, LANES), F32)]
            + [pltpu.VMEM((LANES, HEAD_DIM), F32)] * 3),
        out_shape=jax.ShapeDtypeStruct((bsz * seq, br), F32),
        compiler_params=_params("parallel", "arbitrary"), name="sb_sample")(
            page_table, bias, proj, proj, proj, *([cache_k] * pages), *([cache_v] * pages))


def _router_kernel(x_ref, nw_ref, wr_ref, br_ref, h_ref, r_ref, *, precise):
    hn = _rms(x_ref[...], nw_ref[...])
    h_ref[...] = hn
    logits = _mm(precise)(hn, wr_ref[...]) + br_ref[...]
    lane = lax.broadcasted_iota(jnp.int32, logits.shape, 1)
    big = jnp.int32(LANES)

    def first_max(mask):
        m = jnp.max(jnp.where(mask, logits, NEG_INF), axis=-1, keepdims=True)
        idx = jnp.min(jnp.where(jnp.logical_and(mask, logits == m), lane, big), axis=-1, keepdims=True)
        return m, idx

    is_grp = lane < N_GROUPS
    g_max, grp = first_max(is_grp)
    p_grp = 1.0 / jnp.sum(jnp.where(is_grp, jnp.exp(logits - g_max), 0.0), axis=-1, keepdims=True)
    e_lo = N_GROUPS + grp * EXPERTS_PER_GROUP
    in_grp = jnp.logical_and(lane >= e_lo, lane < e_lo + EXPERTS_PER_GROUP)
    v1, i1 = first_max(in_grp)
    v2, i2 = first_max(jnp.logical_and(in_grp, lane != i1))
    e2 = jnp.exp(v2 - v1)
    w1 = (1.0 / (1.0 + e2)) * p_grp
    w2 = (e2 / (1.0 + e2)) * p_grp
    out = jnp.where(lane == 0, (i1 - N_GROUPS).astype(F32), 0.0)
    out = jnp.where(lane == 1, (i2 - N_GROUPS).astype(F32), out)
    out = jnp.where(lane == 2, w1, out)
    out = jnp.where(lane == 3, w2, out)
    r_ref[...] = out


def _router(x, norm_w, w_router, b_router, *, tm, precise):
    n, d = x.shape
    row = pl.BlockSpec((tm, d), lambda i: (i, 0))
    return pl.pallas_call(
        functools.partial(_router_kernel, precise=precise), grid=(n // tm,),
        in_specs=[row, pl.BlockSpec((1, d), lambda i: (0, 0)), pl.BlockSpec((d, LANES), lambda i: (0, 0)),
                  pl.BlockSpec((1, LANES), lambda i: (0, 0))],
        out_specs=(row, pl.BlockSpec((tm, LANES), lambda i: (i, 0))),
        out_shape=(jax.ShapeDtypeStruct((n, d), F32), jax.ShapeDtypeStruct((n, LANES), F32)),
        compiler_params=_params("parallel"), name="router")(x, norm_w.reshape(1, d), w_router, b_router)


def _new_expert(te_ref, i):
    return jnp.logical_or(i == 0, te_ref[i] != te_ref[jnp.maximum(i - 1, 0)])


def _expert_up_kernel(src_ref, te_ref, nt_ref, x_hbm, wg_ref, wu_ref, rw_ref, o_ref, xbuf, sems, wg_scr, wu_scr):
    i = pl.program_id(0)
    n_t = nt_ref[0]
    tm = o_ref.shape[0]

    def copies(step, start):
        slot = step % 2
        _row_copies(src_ref, step * tm, tm, x_hbm, xbuf.at[slot], sems.at[slot], start)

    @pl.when(i == 0)
    def _():
        copies(i, True)

    @pl.when(i + 1 < n_t)
    def _():
        copies(i + 1, True)

    @pl.when(jnp.logical_and(i < n_t, _new_expert(te_ref, i)))
    def _():
        wg_scr[...] = wg_ref[...].astype(BF16)
        wu_scr[...] = wu_ref[...].astype(BF16)

    @pl.when(i < n_t)
    def _():
        copies(i, False)
        x = xbuf[i % 2].astype(BF16)
        g = _dot(x, wg_scr[...])
        u = _dot(x, wu_scr[...])
        o_ref[...] = ((_silu(g) * u) * rw_ref[...]).astype(o_ref.dtype)

    @pl.when(i >= n_t)
    def _():
        o_ref[...] = jnp.zeros_like(o_ref)


def _expert_down_kernel(te_ref, nt_ref, h_ref, wd_ref, o_ref, wd_scr):
    i = pl.program_id(0)

    @pl.when(jnp.logical_and(i < nt_ref[0], _new_expert(te_ref, i)))
    def _():
        wd_scr[...] = wd_ref[...].astype(BF16)

    @pl.when(i < nt_ref[0])
    def _():
        o_ref[...] = _dot(h_ref[...], wd_scr[...])

    @pl.when(i >= nt_ref[0])
    def _():
        o_ref[...] = jnp.zeros_like(o_ref)


def _experts(h2, src_tok, row_w, tile_expert, n_tiles, w_gate, w_up, w_down, layer):
    r = src_tok.shape[0]
    d = h2.shape[1]
    f = w_gate.shape[-1]
    n_t = r // MOE_TILE
    hid = pl.pallas_call(
        _expert_up_kernel,
        grid_spec=pltpu.PrefetchScalarGridSpec(
            num_scalar_prefetch=3, grid=(n_t,),
            in_specs=[pl.BlockSpec(memory_space=pl.ANY),
                      pl.BlockSpec((None, None, d, f), lambda i, src, te, nt: (layer, te[i], 0, 0)),
                      pl.BlockSpec((None, None, d, f), lambda i, src, te, nt: (layer, te[i], 0, 0)),
                      pl.BlockSpec((MOE_TILE, 1), lambda i, src, te, nt: (i, 0))],
            out_specs=pl.BlockSpec((MOE_TILE, f), lambda i, src, te, nt: (i, 0)),
            scratch_shapes=[pltpu.VMEM((2, MOE_TILE, d), F32), pltpu.SemaphoreType.DMA((2,)),
                            pltpu.VMEM((d, f), BF16), pltpu.VMEM((d, f), BF16)]),
        out_shape=jax.ShapeDtypeStruct((r, f), BF16),
        compiler_params=_params("arbitrary"), name="expert_up")(
            src_tok, tile_expert, n_tiles, h2, w_gate, w_up, row_w)
    return pl.pallas_call(
        _expert_down_kernel,
        grid_spec=pltpu.PrefetchScalarGridSpec(
            num_scalar_prefetch=2, grid=(n_t,),
            in_specs=[pl.BlockSpec((MOE_TILE, f), lambda i, te, nt: (i, 0)),
                      pl.BlockSpec((None, None, f, d), lambda i, te, nt: (layer, te[i], 0, 0))],
            out_specs=pl.BlockSpec((MOE_TILE, d), lambda i, te, nt: (i, 0)),
            scratch_shapes=[pltpu.VMEM((f, d), BF16)]),
        out_shape=jax.ShapeDtypeStruct((r, d), F32),
        compiler_params=_params("arbitrary"), name="expert_down")(tile_expert, n_tiles, hid, w_down)


def _moe_schedule(eid, wsel):
    n = eid.shape[0]
    flat_e = eid.reshape(-1)
    onehot = (flat_e[:, None] == jnp.arange(N_EXPERTS, dtype=jnp.int32)[None, :]).astype(jnp.int32)
    rank = jnp.sum((jnp.cumsum(onehot, axis=0) - onehot) * onehot, axis=1)
    counts = jnp.sum(onehot, axis=0)
    tiles_e = (counts + MOE_TILE - 1) // MOE_TILE
    tiles_end = jnp.cumsum(tiles_e)
    row_start = (tiles_end - tiles_e) * MOE_TILE
    dest = row_start[flat_e] + rank
    n_tiles_max = (2 * n + MOE_TILE - 1) // MOE_TILE + N_EXPERTS
    n_rows = n_tiles_max * MOE_TILE
    src_tok = jnp.zeros((n_rows,), jnp.int32).at[dest].set(jnp.arange(2 * n, dtype=jnp.int32) // 2)
    row_w = jnp.zeros((n_rows,), F32).at[dest].set(wsel.reshape(-1))
    n_tiles = tiles_end[-1]
    tile_ids = jnp.minimum(jnp.arange(n_tiles_max, dtype=jnp.int32), n_tiles - 1)
    tile_expert = jnp.sum((tile_ids[:, None] >= tiles_end[None, :]).astype(jnp.int32), axis=1)
    return dest, src_tok, row_w.reshape(n_rows, 1), tile_expert, n_tiles.reshape(1).astype(jnp.int32)


def _moe_sorted(x, norm_w, w_router, b_router, w_gate, w_up, w_down, layer):
    n = x.shape[0]
    h2, routed = _router(x, norm_w, w_router, b_router, tm=NORM_TILE, precise=False)
    eid = routed[:, 0:2].astype(jnp.int32)
    wsel = routed[:, 2:4]
    dest, src_tok, row_w, tile_expert, n_tiles = _moe_schedule(eid, wsel)
    y_rows = _experts(h2, src_tok, row_w, tile_expert, n_tiles, w_gate, w_up, w_down, layer)
    back = dest.reshape(n, 2).T.reshape(-1)
    return y_rows, back


def _experts_dense_kernel(x_ref, cw_ref, wg_ref, wu_ref, wd_ref, o_ref):
    e = pl.program_id(0)
    fi = pl.program_id(1)

    @pl.when(jnp.logical_and(e == 0, fi == 0))
    def _():
        o_ref[...] = jnp.zeros_like(o_ref)

    x = x_ref[...]
    lane = lax.broadcasted_iota(jnp.int32, (1, LANES), 1)
    cw = jnp.sum(jnp.where(lane == e, cw_ref[...], 0.0), axis=1, keepdims=True)
    hid = (_silu(_dot3(x, wg_ref[...])) * _dot3(x, wu_ref[...])) * cw
    o_ref[...] += _dot3(hid, wd_ref[...])


def _moe_dense(x, norm_w, w_router, b_router, w_gate, w_up, w_down, layer):
    n, d = x.shape
    f = w_gate.shape[-1]
    tf = f // 2
    h2, routed = _router(x, norm_w, w_router, b_router, tm=n, precise=True)
    eid = routed[:, 0:2].astype(jnp.int32)
    combine = jnp.sum(jax.nn.one_hot(eid, LANES, dtype=F32) * routed[:, 2:4, None], axis=1)
    full = lambda shape: pl.BlockSpec(shape, lambda e, fi: (0, 0))
    return pl.pallas_call(
        _experts_dense_kernel, grid=(N_EXPERTS, f // tf),
        in_specs=[full((n, d)), full((n, LANES)),
                  pl.BlockSpec((None, None, d, tf), lambda e, fi: (layer, e, 0, fi)),
                  pl.BlockSpec((None, None, d, tf), lambda e, fi: (layer, e, 0, fi)),
                  pl.BlockSpec((None, None, tf, d), lambda e, fi: (layer, e, fi, 0))],
        out_specs=full((n, d)), out_shape=jax.ShapeDtypeStruct((n, d), F32),
        compiler_params=_params("arbitrary", "arbitrary"), name="experts_dense")(
            h2, combine, w_gate, w_up, w_down)


def _bf16_parts(w):
    hi = w.astype(BF16)
    hi_bits = lax.bitcast_convert_type(hi, jnp.uint16).astype(jnp.uint32) << 16
    lo = (w - lax.bitcast_convert_type(hi_bits, F32)).astype(BF16)
    return hi, lo


def kernel(x_prompt, x_sample, state_hgrn, state_gdn, state_gdn_conv, cache_k, cache_v, page_table, norm_mix, w_in, hgrn_lb, hgrn_norm, gdn_conv, gdn_a_log, gdn_dt_bias, gdn_norm, sb_bias, w_branch, w_out, norm_ffn, w_router_group, b_router_group, w_router_expert, b_router_expert, w_exp_gate, w_exp_up, w_exp_down, final_norm):
    bp, tp, d = x_prompt.shape
    bs, ts, _ = x_sample.shape
    depth = w_in.shape[0]
    n_h = state_hgrn.shape[2]
    br = n_h * HEAD_DIM
    n_p, n_s = bp * tp, bs * ts
    assert n_p % IN_TILE == 0 and n_p % ROW_TILE == 0 and n_p % NORM_TILE == 0
    assert tp % CHUNK == 0 and tp % SB_TQ == 0 and ts % SUBLANES == 0 and n_s % SUBLANES == 0
    assert 2 * n_h <= LANES and br % PROJ_TN == 0
    assert n_h * ts <= LANES and ts & (ts - 1) == 0 and n_h & (n_h - 1) == 0

    heads_per_step = min(HEADS_PER_STEP, n_h)
    ab0 = 8 * br
    ab1 = ab0 + 2 * n_h
    n_main = 11 * br + 3 * d
    gate_col0 = 11 * br
    lb_all = jnp.cumsum(jax.nn.softmax(hgrn_lb.astype(F32), axis=0), axis=0)
    zeros_h = jnp.zeros((bp, n_h, HEAD_DIM, HEAD_DIM), F32)
    zeros_conv = jnp.zeros((bp, CONV_W - 1, 3 * br), F32)

    xp = x_prompt.reshape(n_p, d)
    xs = x_sample.reshape(n_s, d)
    outs_p = [[] for _ in range(5)]
    outs_s = [[] for _ in range(5)]
    hp = _rmsnorm(xp, norm_mix[0], BF16, ROW_TILE)
    hs = _rmsnorm(xs, norm_mix[0], F32, n_s)
    for l in range(depth):
        lb = (lb_all[l] - lb_all[0]).reshape(1, br)
        lb_params = (jnp.log(lb), jnp.log1p(-lb), 1.0 - lb)
        gate_params = jnp.pad(jnp.stack([gdn_a_log[l], gdn_dt_bias[l]]).astype(F32), ((0, 0), (0, LANES - n_h)))
        conv_w = gdn_conv[l].astype(F32)
        bias = sb_bias[l].astype(F32)
        w_router = jnp.pad(jnp.concatenate([w_router_group[l], w_router_expert[l]], axis=1),
                           ((0, 0), (0, LANES - N_GROUPS - N_EXPERTS)))
        b_router = jnp.pad(jnp.concatenate([b_router_group[l], b_router_expert[l]]).astype(F32),
                           (0, LANES - N_GROUPS - N_EXPERTS)).reshape(1, LANES)
        next_norm = norm_mix[l + 1] if l + 1 < depth else final_norm

        w_ab_hi, w_ab_lo = _bf16_parts(w_in[l, :, :ab0])
        w_c_hi, w_c_lo = _bf16_parts(w_in[l, :, ab1:])
        w_gates = jnp.pad(w_in[l, :, ab0:ab1], ((0, 0), (0, LANES - 2 * n_h)))
        proj_ab_cols = _matmul(hp, w_ab_hi, tm=IN_TILE, tn=PROJ_TN, name="in_proj_ab")
        proj_c_cols = _matmul(hp, w_c_hi, tm=IN_TILE, tn=PROJ_TN, name="in_proj_c")
        proj_ab = _matmul(hp, w_gates, tm=IN_TILE, tn=LANES, name="in_proj_gates")
        o_a, hg_p = _hgrn(proj_ab_cols, lb_params, hgrn_norm[l], zeros_h, seq=tp, chunk=CHUNK, precise=False,
                          hp=heads_per_step)
        o_b, gd_p = _gdn(proj_ab_cols, proj_ab, conv_w, zeros_conv, gate_params, gdn_norm[l], zeros_h,
                         seq=tp, chunk=CHUNK, precise=False, hp=heads_per_step)
        o_c = _sb_prompt(proj_c_cols, bias, bsz=bp, seq=tp, n_h=n_h, hp=min(SB_HEADS_PER_STEP, n_h), seg0=0)
        mixed = _merge(o_a, o_b, o_c, w_branch, l, proj_c_cols, gate_col0 - ab0, d, tm=ROW_TILE, precise=False)
        xp = _matmul(mixed, w_out, layer=l, tm=ROW_TILE, tn=PROJ_TN, residual=xp, weight_resident=True,
                     name="out_proj")
        y_rows, back = _moe_sorted(xp, norm_ffn[l], w_router, b_router, w_exp_gate, w_exp_up, w_exp_down, l)
        xp, hp = _moe_add_norm(xp, y_rows, back, next_norm, BF16 if l + 1 < depth else F32)
        outs_p[0].append(hg_p)
        outs_p[1].append(gd_p)
        outs_p[2].append(proj_ab_cols[:, 4 * br:7 * br].reshape(bp, tp, 3 * br)[:, tp - (CONV_W - 1):])
        outs_p[3].append(proj_c_cols[:, br:2 * br].reshape(bp, tp, n_h, HEAD_DIM))
        outs_p[4].append(proj_c_cols[:, 2 * br:3 * br].reshape(bp, tp, n_h, HEAD_DIM))

        proj_s = jnp.concatenate([
            _matmul_precise(hs, w_ab_hi, b_lo=w_ab_lo, tn=PROJ_TN, name="in_proj_s0"),
            _matmul_precise(hs, w_c_hi, b_lo=w_c_lo, tn=PROJ_TN, name="in_proj_s1")], axis=1)
        proj_ab_s = _matmul_precise(hs, w_gates, tn=LANES, name="in_proj_gates_s")
        oa_s, hg_s = _hgrn(proj_s, lb_params, hgrn_norm[l], state_hgrn[l].astype(F32), seq=ts, chunk=ts,
                           precise=True, hp=heads_per_step)
        ob_s, gd_s = _gdn(proj_s, proj_ab_s, conv_w, state_gdn_conv[l].astype(F32), gate_params, gdn_norm[l],
                          state_gdn[l].astype(F32), seq=ts, chunk=ts, precise=True, hp=heads_per_step)
        oc_s = _sb_sample(proj_s, cache_k, cache_v, l, page_table, bias, bsz=bs, seq=ts, n_h=n_h)
        mixed_s = _merge(oa_s, ob_s, oc_s, w_branch, l, proj_s, gate_col0, d, tm=n_s, precise=True)
        xs = _matmul_precise(mixed_s, w_out, layer=l, tn=PROJ_TN, residual=xs, name="out_proj_s")
        y_s = _moe_dense(xs, norm_ffn[l], w_router, b_router, w_exp_gate, w_exp_up, w_exp_down, l)
        xs, hs = _add_norm(xs, y_s, next_norm, F32)
        outs_s[0].append(hg_s)
        outs_s[1].append(gd_s)
        outs_s[2].append(proj_s[:, 4 * br:7 * br].reshape(bs, ts, 3 * br)[:, ts - (CONV_W - 1):])
        outs_s[3].append(proj_s[:, 9 * br:10 * br].reshape(bs, ts, n_h, HEAD_DIM))
        outs_s[4].append(proj_s[:, 10 * br:11 * br].reshape(bs, ts, n_h, HEAD_DIM))

    y_prompt = hp.reshape(bp, tp, d)
    y_sample = hs.reshape(bs, ts, d)
    return (y_prompt, y_sample, *(jnp.stack(o) for o in outs_p), *(jnp.stack(o) for o in outs_s))
```

```python
import functools

import jax
import jax.numpy as jnp
from jax import lax
from jax.experimental import pallas as pl
from jax.experimental.pallas import tpu as pltpu

F32 = jnp.float32
BF16 = jnp.bfloat16

HEAD_DIM = 128
CONV_W = 4
CHUNK = 64
N_GROUPS = 4
EXPERTS_PER_GROUP = 8
N_EXPERTS = N_GROUPS * EXPERTS_PER_GROUP
EPS = 1e-6
LANES = 128
SUBLANES = 8
VMEM_LIMIT_BYTES = 56 * 1024 * 1024
NEG_INF = float("-inf")

IN_TILE = 1024
ROW_TILE = 512
NORM_TILE = 256
PROJ_TN = 512
MOE_TILE = 256
SB_TQ = 256
SB_TK = 128
HEADS_PER_STEP = 8
SB_PAGES_PER_STEP = 4
SB_HEADS_PER_STEP = 4


def _params(*sem):
    return pltpu.CompilerParams(dimension_semantics=sem, vmem_limit_bytes=VMEM_LIMIT_BYTES)


def _split_bf16(x, n):
    parts = []
    r = x
    for i in range(n):
        p = r.astype(BF16)
        parts.append(p)
        if i + 1 < n:
            r = r - p.astype(F32)
    return parts


_NN = (((1,), (0,)), ((), ()))
_NT = (((1,), (1,)), ((), ()))
_TN = (((0,), (0,)), ((), ()))


def _dot(a, b, dims=_NN):
    return lax.dot_general(a, b, dims, preferred_element_type=F32)


def _dotb(a, b, dims=_NN):
    return _dot(a.astype(BF16), b.astype(BF16), dims)


def _dot3(a, b, dims=_NN):
    a_hi, a_lo = _split_bf16(a, 2)
    b_hi, b_lo = _split_bf16(b, 2)
    return _dot(a_hi, b_hi, dims) + (_dot(a_hi, b_lo, dims) + _dot(a_lo, b_hi, dims))


def _mm(precise):
    return _dot3 if precise else _dotb


def _dot01_left(m01, x):
    p0, p1, p2 = _split_bf16(x, 3)
    return _dot(m01, p0) + (_dot(m01, p1) + _dot(m01, p2))


def _dot01_right(x, m01):
    p0, p1, p2 = _split_bf16(x, 3)
    return _dot(p0, m01) + (_dot(p1, m01) + _dot(p2, m01))


def _softplus(x):
    return jnp.maximum(x, 0.0) + jnp.log1p(jnp.exp(-jnp.abs(x)))


def _silu(x):
    return x * jax.nn.sigmoid(x)


def _tri(n, strict=False):
    r = lax.broadcasted_iota(jnp.int32, (n, n), 0)
    c = lax.broadcasted_iota(jnp.int32, (n, n), 1)
    return (r > c) if strict else (r >= c)


def _rms(x, w):
    return x * lax.rsqrt(jnp.mean(x * x, axis=-1, keepdims=True) + EPS) * w


def _gated_head_norm(o, w, z):
    return _rms(o, w) * _silu(z)


def _rmsnorm_kernel(x_ref, w_ref, o_ref):
    o_ref[...] = _rms(x_ref[...], w_ref[...]).astype(o_ref.dtype)


def _rmsnorm(x, w, out_dtype, tm):
    n, d = x.shape
    row = pl.BlockSpec((tm, d), lambda i: (i, 0))
    return pl.pallas_call(
        _rmsnorm_kernel, grid=(n // tm,),
        in_specs=[row, pl.BlockSpec((1, d), lambda i: (0, 0))],
        out_specs=row, out_shape=jax.ShapeDtypeStruct((n, d), out_dtype),
        compiler_params=_params("parallel"), name="rmsnorm")(x, w.reshape(1, d))


def _add_norm_kernel(x_ref, y_ref, w_ref, xo_ref, ho_ref):
    x = x_ref[...] + y_ref[...]
    xo_ref[...] = x
    ho_ref[...] = _rms(x, w_ref[...]).astype(ho_ref.dtype)


def _add_norm(x, y, w, out_dtype):
    n, d = x.shape
    return pl.pallas_call(
        _add_norm_kernel,
        out_shape=(jax.ShapeDtypeStruct((n, d), F32), jax.ShapeDtypeStruct((n, d), out_dtype)),
        compiler_params=pltpu.CompilerParams(vmem_limit_bytes=VMEM_LIMIT_BYTES),
        name="add_norm")(x, y, w.reshape(1, d))


def _row_copies(idx_ref, idx0, n_rows, src_hbm, dst_vmem, sem, start):
    def body(r, carry):
        cp = pltpu.make_async_copy(src_hbm.at[pl.ds(idx_ref[idx0 + r], 1), :], dst_vmem.at[pl.ds(r, 1), :], sem)
        if start:
            cp.start()
        else:
            cp.wait()
        return carry

    lax.fori_loop(0, n_rows, body, 0)


def _moe_add_norm_kernel(back_ref, x_ref, y_hbm, w_ref, xo_ref, ho_ref, ybuf, sems):
    i = pl.program_id(0)
    n_i = pl.num_programs(0)
    tm = x_ref.shape[0]
    n_tok = n_i * tm

    def copies(step, start):
        slot = step % 2
        for k in range(2):
            _row_copies(back_ref, k * n_tok + step * tm, tm, y_hbm, ybuf.at[slot, k], sems.at[slot], start)

    @pl.when(i == 0)
    def _():
        copies(i, True)

    @pl.when(i + 1 < n_i)
    def _():
        copies(i + 1, True)

    copies(i, False)
    slot = i % 2
    x = x_ref[...] + (ybuf[slot, 0] + ybuf[slot, 1])
    xo_ref[...] = x
    ho_ref[...] = _rms(x, w_ref[...]).astype(ho_ref.dtype)


def _moe_add_norm(x, y_rows, back, w, out_dtype):
    n, d = x.shape
    tm = NORM_TILE
    row = pl.BlockSpec((tm, d), lambda i, back: (i, 0))
    return pl.pallas_call(
        _moe_add_norm_kernel,
        grid_spec=pltpu.PrefetchScalarGridSpec(
            num_scalar_prefetch=1, grid=(n // tm,),
            in_specs=[row, pl.BlockSpec(memory_space=pl.ANY), pl.BlockSpec((1, d), lambda i, back: (0, 0))],
            out_specs=(row, row),
            scratch_shapes=[pltpu.VMEM((2, 2, tm, d), F32), pltpu.SemaphoreType.DMA((2,))]),
        out_shape=(jax.ShapeDtypeStruct((n, d), F32), jax.ShapeDtypeStruct((n, d), out_dtype)),
        compiler_params=_params("arbitrary"), name="moe_add_norm")(back, x, y_rows, w.reshape(1, d))


def _mm_kernel(a_ref, b_ref, *rest, has_res):
    o_ref = rest[-1]
    out = _dotb(a_ref[...], b_ref[...])
    if has_res:
        out = out + rest[0][...]
    o_ref[...] = out.astype(o_ref.dtype)


def _mm_wres_kernel(a_ref, b_ref, *rest, has_res):
    o_ref, w_scr = rest[-2], rest[-1]

    @pl.when(pl.program_id(1) == 0)
    def _():
        w_scr[...] = b_ref[...].astype(BF16)

    out = _dot(a_ref[...].astype(BF16), w_scr[...])
    if has_res:
        out = out + rest[0][...]
    o_ref[...] = out.astype(o_ref.dtype)


def _weight_spec(b, layer, k, tn, col_block):
    if b.ndim == 2:
        return pl.BlockSpec((k, tn), lambda *g: (0, col_block(*g)))
    return pl.BlockSpec((None, k, tn), lambda *g: (layer, 0, col_block(*g)))


def _matmul(a, b, *, tm, tn, layer=None, col0=0, n_cols=None, residual=None, weight_resident=False,
            name="matmul"):
    m, k = a.shape
    n_cols = b.shape[-1] if n_cols is None else n_cols
    blk0 = col0 // tn
    if weight_resident:
        grid, row, colb = (n_cols // tn, m // tm), (lambda j, i: i), (lambda j, i: j)
        body, scratch, sem = _mm_wres_kernel, [pltpu.VMEM((k, tn), BF16)], ("parallel", "arbitrary")
    else:
        grid, row, colb = (m // tm, n_cols // tn), (lambda i, j: i), (lambda i, j: j)
        body, scratch, sem = _mm_kernel, [], ("parallel", "parallel")
    in_specs = [pl.BlockSpec((tm, k), lambda *g: (row(*g), 0)),
                _weight_spec(b, layer, k, tn, lambda *g: blk0 + colb(*g))]
    args = [a, b]
    if residual is not None:
        in_specs.append(pl.BlockSpec((tm, tn), lambda *g: (row(*g), colb(*g))))
        args.append(residual)
    return pl.pallas_call(
        functools.partial(body, has_res=residual is not None),
        grid=grid, in_specs=in_specs,
        out_specs=pl.BlockSpec((tm, tn), lambda *g: (row(*g), colb(*g))),
        out_shape=jax.ShapeDtypeStruct((m, n_cols), F32), scratch_shapes=scratch,
        compiler_params=_params(*sem), name=name)(*args)


def _mm_t_kernel(a_ref, bt_ref, o_ref, *, precise):
    o_ref[...] = _mm(precise)(a_ref[...], bt_ref[0], _NT)


def _matmul_t(a, bt, layer, *, tm, tn, row0, n_cols, precise, name):
    m, k = a.shape
    bt_spec = pl.BlockSpec((pl.Element(1), pl.Element(tn), pl.Element(k)),
                           lambda i, j: (layer, pl.multiple_of(row0 + j * tn, SUBLANES), 0))
    return pl.pallas_call(
        functools.partial(_mm_t_kernel, precise=precise), grid=(m // tm, n_cols // tn),
        in_specs=[pl.BlockSpec((tm, k), lambda i, j: (i, 0)), bt_spec],
        out_specs=pl.BlockSpec((tm, tn), lambda i, j: (i, j)),
        out_shape=jax.ShapeDtypeStruct((m, n_cols), F32),
        compiler_params=_params("parallel", "parallel"), name=name)(a, bt)


def _mm_precise_kernel(a_ref, b_ref, *rest, presplit, has_res):
    o_ref = rest[-1]
    if presplit:
        a_hi, a_lo = _split_bf16(a_ref[...], 2)
        b_hi, b_lo = b_ref[...], rest[0][...]
        out = _dot(a_hi, b_hi) + (_dot(a_hi, b_lo) + _dot(a_lo, b_hi))
    else:
        out = _dot3(a_ref[...], b_ref[...])
    if has_res:
        out = out + rest[-2][...]
    o_ref[...] = out


def _matmul_precise(a, b, *, tn, b_lo=None, layer=None, residual=None, name="matmul_precise"):
    m, k = a.shape
    n_cols = b.shape[-1]
    in_specs = [pl.BlockSpec((m, k), lambda j: (0, 0)), _weight_spec(b, layer, k, tn, lambda j: j)]
    args = [a, b]
    if b_lo is not None:
        in_specs.append(_weight_spec(b_lo, layer, k, tn, lambda j: j))
        args.append(b_lo)
    if residual is not None:
        in_specs.append(pl.BlockSpec((m, tn), lambda j: (0, j)))
        args.append(residual)
    return pl.pallas_call(
        functools.partial(_mm_precise_kernel, presplit=b_lo is not None, has_res=residual is not None),
        grid=(n_cols // tn,), in_specs=in_specs,
        out_specs=pl.BlockSpec((m, tn), lambda j: (0, j)),
        out_shape=jax.ShapeDtypeStruct((m, n_cols), F32),
        compiler_params=_params("parallel"), name=name)(*args)


def _merge_kernel(oa_ref, ob_ref, oc_ref, wb_ref, ga_ref, gb_ref, gc_ref, o_ref, *scratch, precise):
    if precise:
        weight = lambda j: wb_ref[j]
    else:
        w_scr, = scratch

        @pl.when(pl.program_id(1) == 0)
        def _():
            w_scr[...] = wb_ref[...].astype(BF16)

        weight = lambda j: w_scr[j]
    acc = None
    for j, (o_r, g_r) in enumerate(((oa_ref, ga_ref), (ob_ref, gb_ref), (oc_ref, gc_ref))):
        term = jax.nn.sigmoid(g_r[...]) * _mm(precise)(o_r[...], weight(j))
        acc = term if acc is None else acc + term
    o_ref[...] = acc.astype(o_ref.dtype)


def _merge(o_a, o_b, o_c, w_branch, layer, proj, gate_col0, d_model, *, tm, precise):
    n, br = o_a.shape
    tn = PROJ_TN
    g0 = gate_col0 // tn
    gstep = d_model // tn
    o_spec = pl.BlockSpec((tm, br), lambda j, i: (i, 0))
    g_specs = [pl.BlockSpec((tm, tn), functools.partial(lambda j, i, b: (i, g0 + b * gstep + j), b=b))
               for b in range(3)]
    return pl.pallas_call(
        functools.partial(_merge_kernel, precise=precise), grid=(d_model // tn, n // tm),
        in_specs=[o_spec, o_spec, o_spec, pl.BlockSpec((None, 3, br, tn), lambda j, i: (layer, 0, 0, j))] + g_specs,
        out_specs=pl.BlockSpec((tm, tn), lambda j, i: (i, j)),
        out_shape=jax.ShapeDtypeStruct((n, d_model), F32 if precise else BF16),
        scratch_shapes=[] if precise else [pltpu.VMEM((3, br, tn), BF16)],
        compiler_params=_params("parallel", "arbitrary"), name="merge")(o_a, o_b, o_c, w_branch, proj, proj, proj)


def _hgrn_kernel(q_ref, f_ref, i_ref, g_ref, llb_ref, l1m_ref, oml_ref, nw_ref, s0_ref,
                 o_ref, sfin_ref, *scratch, chunk, precise, hp):
    c = pl.program_id(2)
    n_c = pl.num_programs(2)
    mm = _mm(precise)
    operand = (lambda x: x) if precise else (lambda x: x.astype(BF16).astype(F32))
    st_scrs, o_scrs, qkvb_scrs = scratch[:hp], scratch[hp:2 * hp], scratch[2 * hp:]
    lanes = [slice(hh * HEAD_DIM, (hh + 1) * HEAD_DIM) for hh in range(hp)]

    @pl.when(c == 0)
    def _():
        for hh in range(hp):
            st_scrs[hh][...] = s0_ref[0, hh].T

    kept = []
    for hh in range(hp):
        af = f_ref[:, lanes[hh]]
        log_sig = jnp.minimum(af, 0.0) - jnp.log1p(jnp.exp(-jnp.abs(af)))
        a = llb_ref[:, lanes[hh]]
        b2 = l1m_ref[:, lanes[hh]] + log_sig
        log_f = jnp.maximum(a, b2) + jnp.log1p(jnp.exp(-jnp.abs(a - b2)))
        k = oml_ref[:, lanes[hh]] * jax.nn.sigmoid(-af)
        q = _silu(q_ref[:, lanes[hh]])
        v = i_ref[:, lanes[hh]]
        b = _dot01_left(_tri(chunk).astype(BF16), log_f)
        st = st_scrs[hh][...]
        o_scrs[hh][...] = mm(q * jnp.exp(b), st, _NT)
        qkvb_scrs[hh][0] = q
        qkvb_scrs[hh][1] = k
        qkvb_scrs[hh][2] = operand(v)
        qkvb_scrs[hh][3] = b
        kept.append((k, v, b, st))
    for s in range(chunk):
        r0 = (s // SUBLANES) * SUBLANES
        rows = r0 + lax.broadcasted_iota(jnp.int32, (chunk - r0, 1), 0)
        for hh in range(hp):
            qkvb, o_scr = qkvb_scrs[hh], o_scrs[hh]
            d = jnp.where(rows >= s, qkvb[3, r0:, :] - qkvb[3, s:s + 1, :], NEG_INF)
            col = jnp.sum(qkvb[0, r0:, :] * qkvb[1, s:s + 1, :] * jnp.exp(d), axis=-1, keepdims=True)
            o_scr[r0:, :] += operand(col) * qkvb[2, s:s + 1, :]
    for hh in range(hp):
        k, v, b, st = kept[hh]
        o_ref[:, lanes[hh]] = _gated_head_norm(o_scrs[hh][...], nw_ref[...], g_ref[:, lanes[hh]]).astype(o_ref.dtype)
        b_end = b[chunk - 1:chunk, :]
        st_new = st * jnp.exp(b_end) + mm(v, k * jnp.exp(b_end - b), _TN)
        st_scrs[hh][...] = st_new

        @pl.when(c == n_c - 1)
        def _(hh=hh, st_new=st_new):
            sfin_ref[0, hh] = st_new.T


def _hgrn(proj, lb_params, norm_w, s0, *, seq, chunk, precise, hp):
    bsz, n_h = s0.shape[:2]
    n_c = seq // chunk
    n_hg = n_h // hp
    wide = hp * HEAD_DIM

    def col(seg):
        return pl.BlockSpec((chunk, wide), lambda b, h, c: (b * n_c + c, seg * n_hg + h))

    par = pl.BlockSpec((1, wide), lambda b, h, c: (0, h))
    state = pl.BlockSpec((1, hp, HEAD_DIM, HEAD_DIM), lambda b, h, c: (b, h, 0, 0))
    in_specs = [col(0), col(1), col(2), col(3), par, par, par,
                pl.BlockSpec((1, HEAD_DIM), lambda b, h, c: (0, 0)), state]
    args = [proj, proj, proj, proj, *lb_params, norm_w.reshape(1, HEAD_DIM), s0]
    return pl.pallas_call(
        functools.partial(_hgrn_kernel, chunk=chunk, precise=precise, hp=hp),
        grid=(bsz, n_hg, n_c), in_specs=in_specs,
        out_specs=(pl.BlockSpec((chunk, wide), lambda b, h, c: (b * n_c + c, h)), state),
        out_shape=(jax.ShapeDtypeStruct((bsz * seq, n_h * HEAD_DIM), F32 if precise else BF16),
                   jax.ShapeDtypeStruct(s0.shape, F32)),
        scratch_shapes=([pltpu.VMEM((HEAD_DIM, HEAD_DIM), F32)] * hp + [pltpu.VMEM((chunk, HEAD_DIM), F32)] * hp
                        + [pltpu.VMEM((4, chunk, HEAD_DIM), F32)] * hp),
        compiler_params=_params("parallel", "parallel", "arbitrary"), name="hgrn")(*args)


def _gdn_kernel(q_ref, k_ref, v_ref, z_ref, ab_ref, cwq_ref, cwk_ref, cwv_ref, cbq_ref, cbk_ref, cbv_ref,
                gp_ref, nw_ref, s0_ref, o_ref, sfin_ref, *scratch, chunk, precise, hp, n_h):
    hg = pl.program_id(1)
    c = pl.program_id(2)
    n_c = pl.num_programs(2)
    mm = _mm(precise)
    hist = CONV_W - 1
    base = SUBLANES - hist
    s_scrs, xp_scrs, x_scrs = scratch[:hp], scratch[hp:2 * hp], scratch[2 * hp:]
    lanes = [slice(hh * HEAD_DIM, (hh + 1) * HEAD_DIM) for hh in range(hp)]

    @pl.when(c == 0)
    def _():
        for hh in range(hp):
            s_scrs[hh][...] = s0_ref[0, hh]
            for j, cb in enumerate((cbq_ref, cbk_ref, cbv_ref)):
                xp_scrs[hh][j, base:SUBLANES, :] = cb[0, :, lanes[hh]]

    def conv(hh, j, x_r, cw_r):
        xp = xp_scrs[hh]
        xp[j, SUBLANES:SUBLANES + chunk, :] = x_r[:, lanes[hh]]
        y = xp[j, base:base + chunk, :] * cw_r[0:1, lanes[hh]]
        for t in range(1, CONV_W):
            y = y + xp[j, base + t:base + t + chunk, :] * cw_r[t:t + 1, lanes[hh]]
        xp[j, base:SUBLANES, :] = xp[j, base + chunk:SUBLANES + chunk, :]
        return _silu(y)

    ab = ab_ref[...]
    g_all = -jnp.exp(gp_ref[0:1, :]) * _softplus(ab + gp_ref[1:2, :])
    cg_all = _dot01_left(_tri(chunk).astype(BF16), g_all)
    sig_ab = jax.nn.sigmoid(ab)
    lane = lax.broadcasted_iota(jnp.int32, (1, LANES), 1)
    incl = _tri(chunk)
    strict = _tri(chunk, strict=True)
    eye = jnp.logical_and(incl, jnp.logical_not(strict))

    kept = []
    for hh in range(hp):
        h = hg * hp + hh
        qc = conv(hh, 0, q_ref, cwq_ref)
        kc = conv(hh, 1, k_ref, cwk_ref)
        vc = conv(hh, 2, v_ref, cwv_ref)
        qn = qc * lax.rsqrt(jnp.sum(qc * qc, axis=-1, keepdims=True) + EPS) * (HEAD_DIM ** -0.5)
        kn = kc * lax.rsqrt(jnp.sum(kc * kc, axis=-1, keepdims=True) + EPS)
        cg = jnp.sum(jnp.where(lane == h, cg_all, 0.0), axis=1, keepdims=True)
        beta = jnp.sum(jnp.where(lane == h + n_h, sig_ab, 0.0), axis=1, keepdims=True)
        cg_row = jnp.sum(jnp.where(eye, jnp.broadcast_to(cg, (chunk, chunk)), 0.0), axis=0, keepdims=True)
        gam = jnp.exp(jnp.where(incl, cg - cg_row, NEG_INF))
        lower = jnp.where(strict, beta * gam * mm(kn, kn, _NT), 0.0)
        s = s_scrs[hh][...]
        e_cg = jnp.exp(cg)
        x_scrs[hh][...] = beta * (vc - e_cg * mm(kn, s))
        kept.append((qn, kn, cg, gam, lower, s, e_cg))
    for j in range(chunk - 1):
        r0 = (j // SUBLANES) * SUBLANES
        for hh in range(hp):
            x_scr, lower = x_scrs[hh], kept[hh][4]
            x_scr[r0:, :] -= lower[r0:, j:j + 1] * x_scr[j:j + 1, :]
    for hh in range(hp):
        qn, kn, cg, gam, lower, s, e_cg = kept[hh]
        u = x_scrs[hh][...]
        qk = mm(qn, kn, _NT) * gam
        o = e_cg * mm(qn, s) + mm(qk, u)
        cg_end = cg[chunk - 1:chunk, :]
        s_new = jnp.exp(cg_end) * s + mm(kn * jnp.exp(cg_end - cg), u, _TN)
        s_scrs[hh][...] = s_new
        o_ref[:, lanes[hh]] = _gated_head_norm(o, nw_ref[...], z_ref[:, lanes[hh]]).astype(o_ref.dtype)

        @pl.when(c == n_c - 1)
        def _(hh=hh, s_new=s_new):
            sfin_ref[0, hh] = s_new


def _gdn(proj, proj_ab, conv_w, conv_buf, gate_params, norm_w, s0, *, seq, chunk, precise, hp):
    bsz, n_h = s0.shape[:2]
    n_c = seq // chunk
    n_hg = n_h // hp
    wide = hp * HEAD_DIM

    def col(seg):
        return pl.BlockSpec((chunk, wide), lambda b, h, c: (b * n_c + c, seg * n_hg + h))

    def cw(seg):
        return pl.BlockSpec((CONV_W, wide), lambda b, h, c: (0, seg * n_hg + h))

    def cb(seg):
        return pl.BlockSpec((1, CONV_W - 1, wide), lambda b, h, c: (b, 0, seg * n_hg + h))

    state = pl.BlockSpec((1, hp, HEAD_DIM, HEAD_DIM), lambda b, h, c: (b, h, 0, 0))
    in_specs = [col(4), col(5), col(6), col(7),
                pl.BlockSpec((chunk, LANES), lambda b, h, c: (b * n_c + c, 0)),
                cw(0), cw(1), cw(2), cb(0), cb(1), cb(2),
                pl.BlockSpec((2, LANES), lambda b, h, c: (0, 0)),
                pl.BlockSpec((1, HEAD_DIM), lambda b, h, c: (0, 0)), state]
    args = [proj, proj, proj, proj, proj_ab, conv_w, conv_w, conv_w, conv_buf, conv_buf, conv_buf,
            gate_params, norm_w.reshape(1, HEAD_DIM), s0]
    return pl.pallas_call(
        functools.partial(_gdn_kernel, chunk=chunk, precise=precise, hp=hp, n_h=n_h),
        grid=(bsz, n_hg, n_c), in_specs=in_specs,
        out_specs=(pl.BlockSpec((chunk, wide), lambda b, h, c: (b * n_c + c, h)), state),
        out_shape=(jax.ShapeDtypeStruct((bsz * seq, n_h * HEAD_DIM), F32 if precise else BF16),
                   jax.ShapeDtypeStruct(s0.shape, F32)),
        scratch_shapes=([pltpu.VMEM((HEAD_DIM, HEAD_DIM), F32)] * hp
                        + [pltpu.VMEM((3, SUBLANES + chunk, HEAD_DIM), F32)] * hp
                        + [pltpu.VMEM((chunk, HEAD_DIM), F32)] * hp),
        compiler_params=_params("parallel", "parallel", "arbitrary"), name="gdn")(*args)


def _sb_block(z, mask, run, u_incl):
    sp = _softplus(z)
    log_stay = jnp.where(mask, -sp, 0.0)
    incl = _dot01_right(log_stay, u_incl)
    later = incl - log_stay + run
    w = jnp.where(mask, jnp.exp((z - sp) + later), 0.0)
    return w, run + incl[:, 0:1]


def _sb_prompt_kernel(bias_ref, q_ref, k_ref, v_ref, o_ref, *scratch, hp):
    hg = pl.program_id(1)
    qi = pl.program_id(2)
    tq = q_ref.shape[0]
    acc_scrs, run_scrs = scratch[:hp], scratch[hp:]
    lanes = [slice(hh * HEAD_DIM, (hh + 1) * HEAD_DIM) for hh in range(hp)]
    qs = [q_ref[:, lanes[hh]].astype(BF16) for hh in range(hp)]
    biases = [bias_ref[hg * hp + hh] for hh in range(hp)]
    scale = HEAD_DIM ** -0.5
    for hh in range(hp):
        acc_scrs[hh][...] = jnp.zeros_like(acc_scrs[hh])
        run_scrs[hh][...] = jnp.zeros_like(run_scrs[hh])
    t_pos = qi * tq + lax.broadcasted_iota(jnp.int32, (tq, 1), 0)
    u_incl = _tri(SB_TK).astype(BF16)
    n_kb = (qi + 1) * (tq // SB_TK)

    def body(i, carry):
        k0 = pl.multiple_of((n_kb - 1 - i) * SB_TK, SB_TK)
        s_pos = k0 + lax.broadcasted_iota(jnp.int32, (1, SB_TK), 1)
        mask = s_pos < t_pos
        for hh in range(hp):
            kb = k_ref[pl.ds(k0, SB_TK), lanes[hh]].astype(BF16)
            vb = v_ref[pl.ds(k0, SB_TK), lanes[hh]].astype(BF16)
            z = _dot(qs[hh], kb, _NT) * scale + biases[hh]
            w, run = _sb_block(z, mask, run_scrs[hh][...], u_incl)
            acc_scrs[hh][...] += _dot(w.astype(BF16), vb)
            run_scrs[hh][...] = run
        return carry

    lax.fori_loop(0, n_kb, body, 0)
    for hh in range(hp):
        o_ref[:, lanes[hh]] = acc_scrs[hh][...].astype(o_ref.dtype)


def _sb_prompt(proj, bias, *, bsz, seq, n_h, hp, seg0):
    n_q = seq // SB_TQ
    n_hg = n_h // hp
    wide = hp * HEAD_DIM
    kv = lambda seg: pl.BlockSpec((seq, wide), lambda b, h, i, bias: (b, seg * n_hg + h))
    return pl.pallas_call(
        functools.partial(_sb_prompt_kernel, hp=hp),
        grid_spec=pltpu.PrefetchScalarGridSpec(
            num_scalar_prefetch=1, grid=(bsz, n_hg, n_q),
            in_specs=[pl.BlockSpec((SB_TQ, wide), lambda b, h, i, bias: (b * n_q + i, seg0 * n_hg + h)),
                      kv(seg0 + 1), kv(seg0 + 2)],
            out_specs=pl.BlockSpec((SB_TQ, wide), lambda b, h, i, bias: (b * n_q + i, h)),
            scratch_shapes=[pltpu.VMEM((SB_TQ, HEAD_DIM), F32)] * hp + [pltpu.VMEM((SB_TQ, 1), F32)] * hp),
        out_shape=jax.ShapeDtypeStruct((bsz * seq, n_h * HEAD_DIM), BF16),
        compiler_params=_params("parallel", "parallel", "arbitrary"), name="sb_prompt")(bias, proj, proj, proj)


def _sb_sample_kernel(pt_ref, bias_ref, q_ref, ko_ref, vo_ref, *rest, n_h, pages):
    kp_refs, vp_refs = rest[:pages], rest[pages:2 * pages]
    o_ref, acc_scr, run_scr, qall_scr, kown_scr, vown_scr = rest[2 * pages:]
    _sb_sample_body(bias_ref, q_ref, ko_ref, vo_ref, kp_refs, vp_refs, o_ref,
                    acc_scr, run_scr, qall_scr, kown_scr, vown_scr, n_h=n_h)
    del pt_ref


def _sb_sample_body(bias_ref, q_ref, ko_ref, vo_ref, kp_refs, vp_refs, o_ref,
                    acc_scr, run_scr, qall_scr, kown_scr, vown_scr, *, n_h):
    j = pl.program_id(1)
    n_j = pl.num_programs(1)
    t = q_ref.shape[0]
    n_pg = kp_refs[0].shape[0]
    t_bits = t.bit_length() - 1
    scale = HEAD_DIM ** -0.5
    lane = lax.broadcasted_iota(jnp.int32, (1, LANES), 1)
    lane_h = lax.shift_right_logical(lane, t_bits)
    lane_t = jnp.bitwise_and(lane, t - 1)
    bias_row = jnp.zeros((1, LANES), F32)
    for h in range(n_h):
        bias_row = jnp.where(lane_h == h, bias_ref[h], bias_row)

    def head_slice(h):
        return slice(h * HEAD_DIM, (h + 1) * HEAD_DIM)

    @pl.when(j == 0)
    def _():
        qall_scr[...] = jnp.zeros_like(qall_scr)
        kown_scr[...] = jnp.zeros_like(kown_scr)
        vown_scr[...] = jnp.zeros_like(vown_scr)
        for h in range(n_h):
            qall_scr[h * t:(h + 1) * t, :] = q_ref[:, head_slice(h)]
            kown_scr[h * t:(h + 1) * t, :] = ko_ref[:, head_slice(h)]
            vown_scr[h * t:(h + 1) * t, :] = vo_ref[:, head_slice(h)]
        z = _dot3(kown_scr[...], qall_scr[...], _NT) * scale + bias_row
        row = lax.broadcasted_iota(jnp.int32, (LANES, 1), 0)
        row_h = lax.shift_right_logical(row, t_bits)
        row_s = jnp.bitwise_and(row, t - 1)
        valid = jnp.logical_and(jnp.logical_and(row_h == lane_h, row_s < lane_t), lane < n_h * t)
        sp = _softplus(z)
        log_stay = jnp.where(valid, -sp, 0.0)
        later_keys =jnp.logical_and(row_h == lane_h, lane_t >= row_s)
        incl = _dot01_left(later_keys.astype(BF16), log_stay)
        w = jnp.where(valid, jnp.exp((z - sp) + (incl - log_stay)), 0.0)
        acc_scr[...] = _dot3(w.T, vown_scr[...])
        run_scr[...] = jnp.sum(log_stay, axis=0, keepdims=True)

    @pl.when(j > 0)
    def _():
        row_h = jnp.bitwise_and(lax.broadcasted_iota(jnp.int32, (n_pg * n_h, 1), 0), n_h - 1)
        valid = row_h == lane_h
        terms = []
        for kp_ref in kp_refs:
            k2 = kp_ref[...].reshape(n_pg * n_h, HEAD_DIM)
            z = _dot3(k2, qall_scr[...], _NT) * scale + bias_row
            sp = _softplus(z)
            log_stay = jnp.where(valid, -sp, 0.0)
            by_key = log_stay.reshape(n_pg, n_h, LANES)
            tail = jnp.zeros((n_h, LANES), F32)
            incl = [None] * n_pg
            for s in reversed(range(n_pg)):
                tail = tail + by_key[s]
                incl[s] = tail
            incl = jnp.stack(incl).reshape(n_pg * n_h, LANES)
            terms.append(((z - sp) + (incl - log_stay), jnp.sum(tail, axis=0, keepdims=True)))
        run = run_scr[...]
        acc = acc_scr[...]
        for (log_w, total), vp_ref in zip(terms, vp_refs):
            w = jnp.where(valid, jnp.exp(log_w + run), 0.0)
            acc = acc + _dot3(w.T, vp_ref[...].reshape(n_pg * n_h, HEAD_DIM))
            run = run + total
        acc_scr[...] = acc
        run_scr[...] = run

    @pl.when(j == n_j - 1)
    def _():
        for h in range(n_h):
            o_ref[:, head_slice(h)] = acc_scr[h * t:(h + 1) * t, :]


def _sb_sample(proj, cache_k, cache_v, layer, page_table, bias, *, bsz, seq, n_h):
    n_pages = page_table.shape[1]
    br = n_h * HEAD_DIM

    def own(seg):
        return pl.BlockSpec((seq, br), lambda b, j, pt, bias: (b, seg))

    pages = SB_PAGES_PER_STEP if n_pages % SB_PAGES_PER_STEP == 0 else 1

    def page(p):
        def index(b, j, pt, bias):
            return (layer, pt[b, n_pages - pages * (jnp.maximum(j, 1) - 1) - 1 - p], 0, 0, 0)
        return pl.BlockSpec((None, None, cache_k.shape[2], n_h, HEAD_DIM), index)

    page_specs = [page(p) for p in range(pages)]
    return pl.pallas_call(
        functools.partial(_sb_sample_kernel, n_h=n_h, pages=pages),
        grid_spec=pltpu.PrefetchScalarGridSpec(
            num_scalar_prefetch=2, grid=(bsz, n_pages // pages + 1),
            in_specs=[own(8), own(9), own(10)] + page_specs + page_specs,
            out_specs=pl.BlockSpec((seq, br), lambda b, j, pt, bias: (b, 0)),
            scratch_shapes=[pltpu.VMEM((LANES, HEAD_DIM), F32), pltpu.VMEM((1, LANES), F32)]
            + [pltpu.VMEM((LANES, HEAD_DIM), F32)] * 3),
        out_shape=jax.ShapeDtypeStruct((bsz * seq, br), F32),
        compiler_params=_params("parallel", "arbitrary"), name="sb_sample")(
            page_table, bias, proj, proj, proj, *([cache_k] * pages), *([cache_v] * pages))


def _router_kernel(x_ref, nw_ref, wr_ref, br_ref, h_ref, r_ref, *, precise):
    hn = _rms(x_ref[...], nw_ref[...])
    h_ref[...] = hn
    logits = _mm(precise)(hn, wr_ref[...]) + br_ref[...]
    lane = lax.broadcasted_iota(jnp.int32, logits.shape, 1)
    big = jnp.int32(LANES)

    def first_max(mask):
        m = jnp.max(jnp.where(mask, logits, NEG_INF), axis=-1, keepdims=True)
        idx = jnp.min(jnp.where(jnp.logical_and(mask, logits == m), lane, big), axis=-1, keepdims=True)
        return m, idx

    is_grp = lane < N_GROUPS
    g_max, grp = first_max(is_grp)
    p_grp = 1.0 / jnp.sum(jnp.where(is_grp, jnp.exp(logits - g_max), 0.0), axis=-1, keepdims=True)
    e_lo = N_GROUPS + grp * EXPERTS_PER_GROUP
    in_grp = jnp.logical_and(lane >= e_lo, lane < e_lo + EXPERTS_PER_GROUP)
    v1, i1 = first_max(in_grp)
    v2, i2 = first_max(jnp.logical_and(in_grp, lane != i1))
    e2 = jnp.exp(v2 - v1)
    w1 = (1.0 / (1.0 + e2)) * p_grp
    w2 = (e2 / (1.0 + e2)) * p_grp
    out = jnp.where(lane == 0, (i1 - N_GROUPS).astype(F32), 0.0)
    out = jnp.where(lane == 1, (i2 - N_GROUPS).astype(F32), out)
    out = jnp.where(lane == 2, w1, out)
    out = jnp.where(lane == 3, w2, out)
    r_ref[...] = out


def _router(x, norm_w, w_router, b_router, *, tm, precise):
    n, d = x.shape
    row = pl.BlockSpec((tm, d), lambda i: (i, 0))
    return pl.pallas_call(
        functools.partial(_router_kernel, precise=precise), grid=(n // tm,),
        in_specs=[row, pl.BlockSpec((1, d), lambda i: (0, 0)), pl.BlockSpec((d, LANES), lambda i: (0, 0)),
                  pl.BlockSpec((1, LANES), lambda i: (0, 0))],
        out_specs=(row, pl.BlockSpec((tm, LANES), lambda i: (i, 0))),
        out_shape=(jax.ShapeDtypeStruct((n, d), F32), jax.ShapeDtypeStruct((n, LANES), F32)),
        compiler_params=_params("parallel"), name="router")(x, norm_w.reshape(1, d), w_router, b_router)


def _new_expert(te_ref, i):
    return jnp.logical_or(i == 0, te_ref[i] != te_ref[jnp.maximum(i - 1, 0)])


def _expert_up_kernel(src_ref, te_ref, nt_ref, x_hbm, wg_ref, wu_ref, rw_ref, o_ref, xbuf, sems, wg_scr, wu_scr):
    i = pl.program_id(0)
    n_t = nt_ref[0]
    tm = o_ref.shape[0]

    def copies(step, start):
        slot = step % 2
        _row_copies(src_ref, step * tm, tm, x_hbm, xbuf.at[slot], sems.at[slot], start)

    @pl.when(i == 0)
    def _():
        copies(i, True)

    @pl.when(i + 1 < n_t)
    def _():
        copies(i + 1, True)

    @pl.when(jnp.logical_and(i < n_t, _new_expert(te_ref, i)))
    def _():
        wg_scr[...] = wg_ref[...].astype(BF16)
        wu_scr[...] = wu_ref[...].astype(BF16)

    @pl.when(i < n_t)
    def _():
        copies(i, False)
        x = xbuf[i % 2].astype(BF16)
        g = _dot(x, wg_scr[...])
        u = _dot(x, wu_scr[...])
        o_ref[...] = ((_silu(g) * u) * rw_ref[...]).astype(o_ref.dtype)

    @pl.when(i >= n_t)
    def _():
        o_ref[...] = jnp.zeros_like(o_ref)


def _expert_down_kernel(te_ref, nt_ref, h_ref, wd_ref, o_ref, wd_scr):
    i = pl.program_id(0)

    @pl.when(jnp.logical_and(i < nt_ref[0], _new_expert(te_ref, i)))
    def _():
        wd_scr[...] = wd_ref[...].astype(BF16)

    @pl.when(i < nt_ref[0])
    def _():
        o_ref[...] = _dot(h_ref[...], wd_scr[...])

    @pl.when(i >= nt_ref[0])
    def _():
        o_ref[...] = jnp.zeros_like(o_ref)


def _experts(h2, src_tok, row_w, tile_expert, n_tiles, w_gate, w_up, w_down, layer):
    r = src_tok.shape[0]
    d = h2.shape[1]
    f = w_gate.shape[-1]
    n_t = r // MOE_TILE
    hid = pl.pallas_call(
        _expert_up_kernel,
        grid_spec=pltpu.PrefetchScalarGridSpec(
            num_scalar_prefetch=3, grid=(n_t,),
            in_specs=[pl.BlockSpec(memory_space=pl.ANY),
                      pl.BlockSpec((None, None, d, f), lambda i, src, te, nt: (layer, te[i], 0, 0)),
                      pl.BlockSpec((None, None, d, f), lambda i, src, te, nt: (layer, te[i], 0, 0)),
                      pl.BlockSpec((MOE_TILE, 1), lambda i, src, te, nt: (i, 0))],
            out_specs=pl.BlockSpec((MOE_TILE, f), lambda i, src, te, nt: (i, 0)),
            scratch_shapes=[pltpu.VMEM((2, MOE_TILE, d), F32), pltpu.SemaphoreType.DMA((2,)),
                            pltpu.VMEM((d, f), BF16), pltpu.VMEM((d, f), BF16)]),
        out_shape=jax.ShapeDtypeStruct((r, f), BF16),
        compiler_params=_params("arbitrary"), name="expert_up")(
            src_tok, tile_expert, n_tiles, h2, w_gate, w_up, row_w)
    return pl.pallas_call(
        _expert_down_kernel,
        grid_spec=pltpu.PrefetchScalarGridSpec(
            num_scalar_prefetch=2, grid=(n_t,),
            in_specs=[pl.BlockSpec((MOE_TILE, f), lambda i, te, nt: (i, 0)),
                      pl.BlockSpec((None, None, f, d), lambda i, te, nt: (layer, te[i], 0, 0))],
            out_specs=pl.BlockSpec((MOE_TILE, d), lambda i, te, nt: (i, 0)),
            scratch_shapes=[pltpu.VMEM((f, d), BF16)]),
        out_shape=jax.ShapeDtypeStruct((r, d), F32),
        compiler_params=_params("arbitrary"), name="expert_down")(tile_expert, n_tiles, hid, w_down)


def _moe_schedule(eid, wsel):
    n = eid.shape[0]
    flat_e = eid.reshape(-1)
    onehot = (flat_e[:, None] == jnp.arange(N_EXPERTS, dtype=jnp.int32)[None, :]).astype(jnp.int32)
    rank = jnp.sum((jnp.cumsum(onehot, axis=0) - onehot) * onehot, axis=1)
    counts = jnp.sum(onehot, axis=0)
    tiles_e = (counts + MOE_TILE - 1) // MOE_TILE
    tiles_end = jnp.cumsum(tiles_e)
    row_start = (tiles_end - tiles_e) * MOE_TILE
    dest = row_start[flat_e] + rank
    n_tiles_max = (2 * n + MOE_TILE - 1) // MOE_TILE + N_EXPERTS
    n_rows = n_tiles_max * MOE_TILE
    src_tok = jnp.zeros((n_rows,), jnp.int32).at[dest].set(jnp.arange(2 * n, dtype=jnp.int32) // 2)
    row_w = jnp.zeros((n_rows,), F32).at[dest].set(wsel.reshape(-1))
    n_tiles = tiles_end[-1]
    tile_ids = jnp.minimum(jnp.arange(n_tiles_max, dtype=jnp.int32), n_tiles - 1)
    tile_expert = jnp.sum((tile_ids[:, None] >= tiles_end[None, :]).astype(jnp.int32), axis=1)
    return dest, src_tok, row_w.reshape(n_rows, 1), tile_expert, n_tiles.reshape(1).astype(jnp.int32)


def _moe_sorted(x, norm_w, w_router, b_router, w_gate, w_up, w_down, layer):
    n = x.shape[0]
    h2, routed = _router(x, norm_w, w_router, b_router, tm=NORM_TILE, precise=False)
    eid = routed[:, 0:2].astype(jnp.int32)
    wsel = routed[:, 2:4]
    dest, src_tok, row_w, tile_expert, n_tiles = _moe_schedule(eid, wsel)
    y_rows = _experts(h2, src_tok, row_w, tile_expert, n_tiles, w_gate, w_up, w_down, layer)
    back = dest.reshape(n, 2).T.reshape(-1)
    return y_rows, back


def _experts_dense_kernel(x_ref, cw_ref, wg_ref, wu_ref, wd_ref, o_ref):
    e = pl.program_id(0)
    fi = pl.program_id(1)

    @pl.when(jnp.logical_and(e == 0, fi == 0))
    def _():
        o_ref[...] = jnp.zeros_like(o_ref)

    x = x_ref[...]
    lane = lax.broadcasted_iota(jnp.int32, (1, LANES), 1)
    cw = jnp.sum(jnp.where(lane == e, cw_ref[...], 0.0), axis=1, keepdims=True)
    hid = (_silu(_dot3(x, wg_ref[...])) * _dot3(x, wu_ref[...])) * cw
    o_ref[...] += _dot3(hid, wd_ref[...])


def _moe_dense(x, norm_w, w_router, b_router, w_gate, w_up, w_down, layer):
    n, d = x.shape
    f = w_gate.shape[-1]
    tf = f // 2
    h2, routed = _router(x, norm_w, w_router, b_router, tm=n, precise=True)
    eid = routed[:, 0:2].astype(jnp.int32)
    combine = jnp.sum(jax.nn.one_hot(eid, LANES, dtype=F32) * routed[:, 2:4, None], axis=1)
    full = lambda shape: pl.BlockSpec(shape, lambda e, fi: (0, 0))
    return pl.pallas_call(
        _experts_dense_kernel, grid=(N_EXPERTS, f // tf),
        in_specs=[full((n, d)), full((n, LANES)),
                  pl.BlockSpec((None, None, d, tf), lambda e, fi: (layer, e, 0, fi)),
                  pl.BlockSpec((None, None, d, tf), lambda e, fi: (layer, e, 0, fi)),
                  pl.BlockSpec((None, None, tf, d), lambda e, fi: (layer, e, fi, 0))],
        out_specs=full((n, d)), out_shape=jax.ShapeDtypeStruct((n, d), F32),
        compiler_params=_params("arbitrary", "arbitrary"), name="experts_dense")(
            h2, combine, w_gate, w_up, w_down)


def kernel(x_prompt, x_sample, state_hgrn, state_gdn, state_gdn_conv, cache_k, cache_v, page_table, norm_mix, w_in, hgrn_lb, hgrn_norm, gdn_conv, gdn_a_log, gdn_dt_bias, gdn_norm, sb_bias, w_branch, w_out, norm_ffn, w_router_group, b_router_group, w_router_expert, b_router_expert, w_exp_gate, w_exp_up, w_exp_down, final_norm):
    bp, tp, d = x_prompt.shape
    bs, ts, _ = x_sample.shape
    depth = w_in.shape[0]
    n_h = state_hgrn.shape[2]
    br = n_h * HEAD_DIM
    n_p, n_s = bp * tp, bs * ts
    assert n_p % IN_TILE == 0 and n_p % ROW_TILE == 0 and n_p % NORM_TILE == 0
    assert tp % CHUNK == 0 and tp % SB_TQ == 0 and ts % SUBLANES == 0 and n_s % SUBLANES == 0
    assert 2 * n_h <= LANES and br % PROJ_TN == 0
    assert n_h * ts <= LANES and ts & (ts - 1) == 0 and n_h & (n_h - 1) == 0

    heads_per_step = min(HEADS_PER_STEP, n_h)
    ab0 = 8 * br
    ab1 = ab0 + 2 * n_h
    n_main = 11 * br + 3 * d
    gate_col0 = 11 * br
    w_in_t = jnp.swapaxes(w_in, 1, 2)
    lb_all = jnp.cumsum(jax.nn.softmax(hgrn_lb.astype(F32), axis=0), axis=0)
    zeros_h = jnp.zeros((bp, n_h, HEAD_DIM, HEAD_DIM), F32)
    zeros_conv = jnp.zeros((bp, CONV_W - 1, 3 * br), F32)

    xp = x_prompt.reshape(n_p, d)
    xs = x_sample.reshape(n_s, d)
    outs_p = [[] for _ in range(5)]
    outs_s = [[] for _ in range(5)]
    hp = _rmsnorm(xp, norm_mix[0], BF16, ROW_TILE)
    hs = _rmsnorm(xs, norm_mix[0], F32, n_s)
    for l in range(depth):
        lb = (lb_all[l] - lb_all[0]).reshape(1, br)
        lb_params = (jnp.log(lb), jnp.log1p(-lb), 1.0 - lb)
        gate_params = jnp.pad(jnp.stack([gdn_a_log[l], gdn_dt_bias[l]]).astype(F32), ((0, 0), (0, LANES - n_h)))
        conv_w = gdn_conv[l].astype(F32)
        bias = sb_bias[l].astype(F32)
        w_router = jnp.pad(jnp.concatenate([w_router_group[l], w_router_expert[l]], axis=1),
                           ((0, 0), (0, LANES - N_GROUPS - N_EXPERTS)))
        b_router = jnp.pad(jnp.concatenate([b_router_group[l], b_router_expert[l]]).astype(F32),
                           (0, LANES - N_GROUPS - N_EXPERTS)).reshape(1, LANES)
        next_norm = norm_mix[l + 1] if l + 1 < depth else final_norm

        proj_ab_cols = _matmul_t(hp, w_in_t, l, tm=IN_TILE, tn=PROJ_TN, row0=0, n_cols=ab0, precise=False,
                                 name="in_proj_ab")
        proj_c_cols = _matmul_t(hp, w_in_t, l, tm=IN_TILE, tn=PROJ_TN, row0=ab1, n_cols=n_main - ab0,
                                precise=False, name="in_proj_c")
        proj_ab = _matmul_t(hp, w_in_t, l, tm=IN_TILE, tn=LANES, row0=ab0, n_cols=LANES, precise=False,
                            name="in_proj_gates")
        o_a, hg_p = _hgrn(proj_ab_cols, lb_params, hgrn_norm[l], zeros_h, seq=tp, chunk=CHUNK, precise=False,
                          hp=heads_per_step)
        o_b, gd_p = _gdn(proj_ab_cols, proj_ab, conv_w, zeros_conv, gate_params, gdn_norm[l], zeros_h,
                         seq=tp, chunk=CHUNK, precise=False, hp=heads_per_step)
        o_c = _sb_prompt(proj_c_cols, bias, bsz=bp, seq=tp, n_h=n_h, hp=min(SB_HEADS_PER_STEP, n_h), seg0=0)
        mixed = _merge(o_a, o_b, o_c, w_branch, l, proj_c_cols, gate_col0 - ab0, d, tm=ROW_TILE, precise=False)
        xp = _matmul(mixed, w_out, layer=l, tm=ROW_TILE, tn=PROJ_TN, residual=xp, weight_resident=True,
                     name="out_proj")
        y_rows, back = _moe_sorted(xp, norm_ffn[l], w_router, b_router, w_exp_gate, w_exp_up, w_exp_down, l)
        xp, hp = _moe_add_norm(xp, y_rows, back, next_norm, BF16 if l + 1 < depth else F32)
        outs_p[0].append(hg_p)
        outs_p[1].append(gd_p)
        outs_p[2].append(proj_ab_cols[:, 4 * br:7 * br].reshape(bp, tp, 3 * br)[:, tp - (CONV_W - 1):])
        outs_p[3].append(proj_c_cols[:, br:2 * br].reshape(bp, tp, n_h, HEAD_DIM))
        outs_p[4].append(proj_c_cols[:, 2 * br:3 * br].reshape(bp, tp, n_h, HEAD_DIM))

        proj_s = jnp.concatenate([
            _matmul_t(hs, w_in_t, l, tm=n_s, tn=PROJ_TN, row0=0, n_cols=ab0, precise=True, name="in_proj_s0"),
            _matmul_t(hs, w_in_t, l, tm=n_s, tn=PROJ_TN, row0=ab1, n_cols=n_main - ab0, precise=True,
                      name="in_proj_s1")], axis=1)
        proj_ab_s = _matmul_t(hs, w_in_t, l, tm=n_s, tn=LANES, row0=ab0, n_cols=LANES, precise=True,
                              name="in_proj_gates_s")
        oa_s, hg_s = _hgrn(proj_s, lb_params, hgrn_norm[l], state_hgrn[l].astype(F32), seq=ts, chunk=ts,
                           precise=True, hp=heads_per_step)
        ob_s, gd_s = _gdn(proj_s, proj_ab_s, conv_w, state_gdn_conv[l].astype(F32), gate_params, gdn_norm[l],
                          state_gdn[l].astype(F32), seq=ts, chunk=ts, precise=True, hp=heads_per_step)
        oc_s = _sb_sample(proj_s, cache_k, cache_v, l, page_table, bias, bsz=bs, seq=ts, n_h=n_h)
        mixed_s = _merge(oa_s, ob_s, oc_s, w_branch, l, proj_s, gate_col0, d, tm=n_s, precise=True)
        xs = _matmul_precise(mixed_s, w_out, layer=l, tn=PROJ_TN, residual=xs, name="out_proj_s")
        y_s = _moe_dense(xs, norm_ffn[l], w_router, b_router, w_exp_gate, w_exp_up, w_exp_down, l)
        xs, hs = _add_norm(xs, y_s, next_norm, F32)
        outs_s[0].append(hg_s)
        outs_s[1].append(gd_s)
        outs_s[2].append(proj_s[:, 4 * br:7 * br].reshape(bs, ts, 3 * br)[:, ts - (CONV_W - 1):])
        outs_s[3].append(proj_s[:, 9 * br:10 * br].reshape(bs, ts, n_h, HEAD_DIM))
        outs_s[4].append(proj_s[:, 10 * br:11 * br].reshape(bs, ts, n_h, HEAD_DIM))

    y_prompt = hp.reshape(bp, tp, d)
    y_sample = hs.reshape(bs, ts, d)
    return (y_prompt, y_sample, *(jnp.stack(o) for o in outs_p), *(jnp.stack(o) for o in outs_s))
```

```python
import functools

import jax
import jax.numpy as jnp
from jax import lax
from jax.experimental import pallas as pl
from jax.experimental.pallas import tpu as pltpu

F32 = jnp.float32
BF16 = jnp.bfloat16

HEAD_DIM = 128
CONV_W = 4
CHUNK = 64
N_GROUPS = 4
EXPERTS_PER_GROUP = 8
N_EXPERTS = N_GROUPS * EXPERTS_PER_GROUP
EPS = 1e-6
LANES = 128
SUBLANES = 8
VMEM_LIMIT_BYTES = 56 * 1024 * 1024
NEG_INF = float("-inf")

IN_TILE = 1024
ROW_TILE = 512
NORM_TILE = 256
PROJ_TN = 512
MOE_TILE = 256
SB_TQ = 512
SB_TK = 128
HEADS_PER_STEP = 8
SB_PAGES_PER_STEP = 4
SB_HEADS_PER_STEP = 8


def _params(*sem):
    return pltpu.CompilerParams(dimension_semantics=sem, vmem_limit_bytes=VMEM_LIMIT_BYTES)


def _split_bf16(x, n):
    parts = []
    r = x
    for i in range(n):
        p = r.astype(BF16)
        parts.append(p)
        if i + 1 < n:
            r = r - p.astype(F32)
    return parts


_NN = (((1,), (0,)), ((), ()))
_NT = (((1,), (1,)), ((), ()))
_TN = (((0,), (0,)), ((), ()))


def _dot(a, b, dims=_NN):
    return lax.dot_general(a, b, dims, preferred_element_type=F32)


def _dotb(a, b, dims=_NN):
    return _dot(a.astype(BF16), b.astype(BF16), dims)


def _dot3(a, b, dims=_NN):
    a_hi, a_lo = _split_bf16(a, 2)
    b_hi, b_lo = _split_bf16(b, 2)
    return _dot(a_hi, b_hi, dims) + (_dot(a_hi, b_lo, dims) + _dot(a_lo, b_hi, dims))


def _mm(precise):
    return _dot3 if precise else _dotb


def _dot01_left(m01, x):
    p0, p1, p2 = _split_bf16(x, 3)
    return _dot(m01, p0) + (_dot(m01, p1) + _dot(m01, p2))


def _dot01_right(x, m01):
    p0, p1, p2 = _split_bf16(x, 3)
    return _dot(p0, m01) + (_dot(p1, m01) + _dot(p2, m01))


def _softplus(x):
    return jnp.maximum(x, 0.0) + jnp.log1p(jnp.exp(-jnp.abs(x)))


def _silu(x):
    return x * jax.nn.sigmoid(x)


def _tri(n, strict=False):
    r = lax.broadcasted_iota(jnp.int32, (n, n), 0)
    c = lax.broadcasted_iota(jnp.int32, (n, n), 1)
    return (r > c) if strict else (r >= c)


def _rms(x, w):
    return x * lax.rsqrt(jnp.mean(x * x, axis=-1, keepdims=True) + EPS) * w


def _gated_head_norm(o, w, z):
    return _rms(o, w) * _silu(z)


def _rmsnorm_kernel(x_ref, w_ref, o_ref):
    o_ref[...] = _rms(x_ref[...], w_ref[...]).astype(o_ref.dtype)


def _rmsnorm(x, w, out_dtype, tm):
    n, d = x.shape
    row = pl.BlockSpec((tm, d), lambda i: (i, 0))
    return pl.pallas_call(
        _rmsnorm_kernel, grid=(n // tm,),
        in_specs=[row, pl.BlockSpec((1, d), lambda i: (0, 0))],
        out_specs=row, out_shape=jax.ShapeDtypeStruct((n, d), out_dtype),
        compiler_params=_params("parallel"), name="rmsnorm")(x, w.reshape(1, d))


def _add_norm_kernel(x_ref, y_ref, w_ref, xo_ref, ho_ref):
    x = x_ref[...] + y_ref[...]
    xo_ref[...] = x
    ho_ref[...] = _rms(x, w_ref[...]).astype(ho_ref.dtype)


def _add_norm(x, y, w, out_dtype):
    n, d = x.shape
    return pl.pallas_call(
        _add_norm_kernel,
        out_shape=(jax.ShapeDtypeStruct((n, d), F32), jax.ShapeDtypeStruct((n, d), out_dtype)),
        compiler_params=pltpu.CompilerParams(vmem_limit_bytes=VMEM_LIMIT_BYTES),
        name="add_norm")(x, y, w.reshape(1, d))


def _row_copies(idx_ref, idx0, n_rows, src_hbm, dst_vmem, sem, start):
    def body(r, carry):
        cp = pltpu.make_async_copy(src_hbm.at[pl.ds(idx_ref[idx0 + r], 1), :], dst_vmem.at[pl.ds(r, 1), :], sem)
        if start:
            cp.start()
        else:
            cp.wait()
        return carry

    lax.fori_loop(0, n_rows, body, 0, unroll=8)


def _moe_add_norm_kernel(back_ref, x_ref, y_hbm, w_ref, xo_ref, ho_ref, ybuf, sems):
    i = pl.program_id(0)
    n_i = pl.num_programs(0)
    tm = x_ref.shape[0]
    n_tok = n_i * tm

    def copies(step, start):
        slot = step % 2
        for k in range(2):
            _row_copies(back_ref, k * n_tok + step * tm, tm, y_hbm, ybuf.at[slot, k], sems.at[slot], start)

    @pl.when(i == 0)
    def _():
        copies(i, True)

    @pl.when(i + 1 < n_i)
    def _():
        copies(i + 1, True)

    copies(i, False)
    slot = i % 2
    x = x_ref[...] + (ybuf[slot, 0] + ybuf[slot, 1])
    xo_ref[...] = x
    ho_ref[...] = _rms(x, w_ref[...]).astype(ho_ref.dtype)


def _moe_add_norm(x, y_rows, back, w, out_dtype):
    n, d = x.shape
    tm = NORM_TILE
    row = pl.BlockSpec((tm, d), lambda i, back: (i, 0))
    return pl.pallas_call(
        _moe_add_norm_kernel,
        grid_spec=pltpu.PrefetchScalarGridSpec(
            num_scalar_prefetch=1, grid=(n // tm,),
            in_specs=[row, pl.BlockSpec(memory_space=pl.ANY), pl.BlockSpec((1, d), lambda i, back: (0, 0))],
            out_specs=(row, row),
            scratch_shapes=[pltpu.VMEM((2, 2, tm, d), F32), pltpu.SemaphoreType.DMA((2,))]),
        out_shape=(jax.ShapeDtypeStruct((n, d), F32), jax.ShapeDtypeStruct((n, d), out_dtype)),
        compiler_params=_params("arbitrary"), name="moe_add_norm")(back, x, y_rows, w.reshape(1, d))


def _mm_kernel(a_ref, b_ref, *rest, has_res):
    o_ref = rest[-1]
    out = _dotb(a_ref[...], b_ref[...])
    if has_res:
        out = out + rest[0][...]
    o_ref[...] = out.astype(o_ref.dtype)


def _mm_wres_kernel(a_ref, b_ref, *rest, has_res):
    o_ref, w_scr = rest[-2], rest[-1]

    @pl.when(pl.program_id(1) == 0)
    def _():
        w_scr[...] = b_ref[...].astype(BF16)

    out = _dot(a_ref[...].astype(BF16), w_scr[...])
    if has_res:
        out = out + rest[0][...]
    o_ref[...] = out.astype(o_ref.dtype)


def _weight_spec(b, layer, k, tn, col_block):
    if b.ndim == 2:
        return pl.BlockSpec((k, tn), lambda *g: (0, col_block(*g)))
    return pl.BlockSpec((None, k, tn), lambda *g: (layer, 0, col_block(*g)))


def _matmul(a, b, *, tm, tn, layer=None, col0=0, n_cols=None, residual=None, weight_resident=False,
            name="matmul"):
    m, k = a.shape
    n_cols = b.shape[-1] if n_cols is None else n_cols
    blk0 = col0 // tn
    if weight_resident:
        grid, row, colb = (n_cols // tn, m // tm), (lambda j, i: i), (lambda j, i: j)
        body, scratch, sem = _mm_wres_kernel, [pltpu.VMEM((k, tn), BF16)], ("parallel", "arbitrary")
    else:
        grid, row, colb = (m // tm, n_cols // tn), (lambda i, j: i), (lambda i, j: j)
        body, scratch, sem = _mm_kernel, [], ("parallel", "parallel")
    in_specs = [pl.BlockSpec((tm, k), lambda *g: (row(*g), 0)),
                _weight_spec(b, layer, k, tn, lambda *g: blk0 + colb(*g))]
    args = [a, b]
    if residual is not None:
        in_specs.append(pl.BlockSpec((tm, tn), lambda *g: (row(*g), colb(*g))))
        args.append(residual)
    return pl.pallas_call(
        functools.partial(body, has_res=residual is not None),
        grid=grid, in_specs=in_specs,
        out_specs=pl.BlockSpec((tm, tn), lambda *g: (row(*g), colb(*g))),
        out_shape=jax.ShapeDtypeStruct((m, n_cols), F32), scratch_shapes=scratch,
        compiler_params=_params(*sem), name=name)(*args)


def _mm_t_kernel(a_ref, bt_ref, o_ref, *, precise):
    o_ref[...] = _mm(precise)(a_ref[...], bt_ref[0], _NT)


def _matmul_t(a, bt, layer, *, tm, tn, row0, n_cols, precise, name):
    m, k = a.shape
    bt_spec = pl.BlockSpec((pl.Element(1), pl.Element(tn), pl.Element(k)),
                           lambda i, j: (layer, pl.multiple_of(row0 + j * tn, SUBLANES), 0))
    return pl.pallas_call(
        functools.partial(_mm_t_kernel, precise=precise), grid=(m // tm, n_cols // tn),
        in_specs=[pl.BlockSpec((tm, k), lambda i, j: (i, 0)), bt_spec],
        out_specs=pl.BlockSpec((tm, tn), lambda i, j: (i, j)),
        out_shape=jax.ShapeDtypeStruct((m, n_cols), F32),
        compiler_params=_params("parallel", "parallel"), name=name)(a, bt)


def _mm_precise_kernel(a_ref, b_ref, *rest, presplit, has_res):
    o_ref = rest[-1]
    if presplit:
        a_hi, a_lo = _split_bf16(a_ref[...], 2)
        b_hi, b_lo = b_ref[...], rest[0][...]
        out = _dot(a_hi, b_hi) + (_dot(a_hi, b_lo) + _dot(a_lo, b_hi))
    else:
        out = _dot3(a_ref[...], b_ref[...])
    if has_res:
        out = out + rest[-2][...]
    o_ref[...] = out


def _matmul_precise(a, b, *, tn, b_lo=None, layer=None, residual=None, name="matmul_precise"):
    m, k = a.shape
    n_cols = b.shape[-1]
    in_specs = [pl.BlockSpec((m, k), lambda j: (0, 0)), _weight_spec(b, layer, k, tn, lambda j: j)]
    args = [a, b]
    if b_lo is not None:
        in_specs.append(_weight_spec(b_lo, layer, k, tn, lambda j: j))
        args.append(b_lo)
    if residual is not None:
        in_specs.append(pl.BlockSpec((m, tn), lambda j: (0, j)))
        args.append(residual)
    return pl.pallas_call(
        functools.partial(_mm_precise_kernel, presplit=b_lo is not None, has_res=residual is not None),
        grid=(n_cols // tn,), in_specs=in_specs,
        out_specs=pl.BlockSpec((m, tn), lambda j: (0, j)),
        out_shape=jax.ShapeDtypeStruct((m, n_cols), F32),
        compiler_params=_params("parallel"), name=name)(*args)


def _merge_kernel(oa_ref, ob_ref, oc_ref, wb_ref, ga_ref, gb_ref, gc_ref, o_ref, *scratch, precise):
    if precise:
        weight = lambda j: wb_ref[j]
    else:
        w_scr, = scratch

        @pl.when(pl.program_id(1) == 0)
        def _():
            w_scr[...] = wb_ref[...].astype(BF16)

        weight = lambda j: w_scr[j]
    acc = None
    for j, (o_r, g_r) in enumerate(((oa_ref, ga_ref), (ob_ref, gb_ref), (oc_ref, gc_ref))):
        term = jax.nn.sigmoid(g_r[...]) * _mm(precise)(o_r[...], weight(j))
        acc = term if acc is None else acc + term
    o_ref[...] = acc.astype(o_ref.dtype)


def _merge(o_a, o_b, o_c, w_branch, layer, proj, gate_col0, d_model, *, tm, precise):
    n, br = o_a.shape
    tn = PROJ_TN
    g0 = gate_col0 // tn
    gstep = d_model // tn
    o_spec = pl.BlockSpec((tm, br), lambda j, i: (i, 0))
    g_specs = [pl.BlockSpec((tm, tn), functools.partial(lambda j, i, b: (i, g0 + b * gstep + j), b=b))
               for b in range(3)]
    return pl.pallas_call(
        functools.partial(_merge_kernel, precise=precise), grid=(d_model // tn, n // tm),
        in_specs=[o_spec, o_spec, o_spec, pl.BlockSpec((None, 3, br, tn), lambda j, i: (layer, 0, 0, j))] + g_specs,
        out_specs=pl.BlockSpec((tm, tn), lambda j, i: (i, j)),
        out_shape=jax.ShapeDtypeStruct((n, d_model), F32 if precise else BF16),
        scratch_shapes=[] if precise else [pltpu.VMEM((3, br, tn), BF16)],
        compiler_params=_params("parallel", "arbitrary"), name="merge")(o_a, o_b, o_c, w_branch, proj, proj, proj)


def _hgrn_kernel(q_ref, f_ref, i_ref, g_ref, llb_ref, l1m_ref, oml_ref, nw_ref, s0_ref,
                 o_ref, sfin_ref, *scratch, chunk, precise, hp):
    c = pl.program_id(2)
    n_c = pl.num_programs(2)
    mm = _mm(precise)
    operand = (lambda x: x) if precise else (lambda x: x.astype(BF16).astype(F32))
    st_scrs, o_scrs, qkvb_scrs = scratch[:hp], scratch[hp:2 * hp], scratch[2 * hp:]
    lanes = [slice(hh * HEAD_DIM, (hh + 1) * HEAD_DIM) for hh in range(hp)]

    @pl.when(c == 0)
    def _():
        for hh in range(hp):
            st_scrs[hh][...] = s0_ref[0, hh].T

    kept = []
    for hh in range(hp):
        af = f_ref[:, lanes[hh]]
        log_sig = jnp.minimum(af, 0.0) - jnp.log1p(jnp.exp(-jnp.abs(af)))
        a = llb_ref[:, lanes[hh]]
        b2 = l1m_ref[:, lanes[hh]] + log_sig
        log_f = jnp.maximum(a, b2) + jnp.log1p(jnp.exp(-jnp.abs(a - b2)))
        k = oml_ref[:, lanes[hh]] * jax.nn.sigmoid(-af)
        q = _silu(q_ref[:, lanes[hh]])
        v = i_ref[:, lanes[hh]]
        b = _dot01_left(_tri(chunk).astype(BF16), log_f)
        st = st_scrs[hh][...]
        o_scrs[hh][...] = mm(q * jnp.exp(b), st, _NT)
        qkvb_scrs[hh][0] = q
        qkvb_scrs[hh][1] = k
        qkvb_scrs[hh][2] = operand(v)
        qkvb_scrs[hh][3] = b
        kept.append((k, v, b, st))
    for s in range(chunk):
        r0 = (s // SUBLANES) * SUBLANES
        rows = r0 + lax.broadcasted_iota(jnp.int32, (chunk - r0, 1), 0)
        for hh in range(hp):
            qkvb, o_scr = qkvb_scrs[hh], o_scrs[hh]
            d = jnp.where(rows >= s, qkvb[3, r0:, :] - qkvb[3, s:s + 1, :], NEG_INF)
            col = jnp.sum(qkvb[0, r0:, :] * qkvb[1, s:s + 1, :] * jnp.exp(d), axis=-1, keepdims=True)
            o_scr[r0:, :] += operand(col) * qkvb[2, s:s + 1, :]
    for hh in range(hp):
        k, v, b, st = kept[hh]
        o_ref[:, lanes[hh]] = _gated_head_norm(o_scrs[hh][...], nw_ref[...], g_ref[:, lanes[hh]]).astype(o_ref.dtype)
        b_end = b[chunk - 1:chunk, :]
        st_new = st * jnp.exp(b_end) + mm(v, k * jnp.exp(b_end - b), _TN)
        st_scrs[hh][...] = st_new

        @pl.when(c == n_c - 1)
        def _(hh=hh, st_new=st_new):
            sfin_ref[0, hh] = st_new.T


def _hgrn(proj, lb_params, norm_w, s0, *, seq, chunk, precise, hp):
    bsz, n_h = s0.shape[:2]
    n_c = seq // chunk
    n_hg = n_h // hp
    wide = hp * HEAD_DIM

    def col(seg):
        return pl.BlockSpec((chunk, wide), lambda b, h, c: (b * n_c + c, seg * n_hg + h))

    par = pl.BlockSpec((1, wide), lambda b, h, c: (0, h))
    state = pl.BlockSpec((1, hp, HEAD_DIM, HEAD_DIM), lambda b, h, c: (b, h, 0, 0))
    in_specs = [col(0), col(1), col(2), col(3), par, par, par,
                pl.BlockSpec((1, HEAD_DIM), lambda b, h, c: (0, 0)), state]
    args = [proj, proj, proj, proj, *lb_params, norm_w.reshape(1, HEAD_DIM), s0]
    return pl.pallas_call(
        functools.partial(_hgrn_kernel, chunk=chunk, precise=precise, hp=hp),
        grid=(bsz, n_hg, n_c), in_specs=in_specs,
        out_specs=(pl.BlockSpec((chunk, wide), lambda b, h, c: (b * n_c + c, h)), state),
        out_shape=(jax.ShapeDtypeStruct((bsz * seq, n_h * HEAD_DIM), F32 if precise else BF16),
                   jax.ShapeDtypeStruct(s0.shape, F32)),
        scratch_shapes=([pltpu.VMEM((HEAD_DIM, HEAD_DIM), F32)] * hp + [pltpu.VMEM((chunk, HEAD_DIM), F32)] * hp
                        + [pltpu.VMEM((4, chunk, HEAD_DIM), F32)] * hp),
        compiler_params=_params("parallel", "parallel", "arbitrary"), name="hgrn")(*args)


def _gdn_kernel(q_ref, k_ref, v_ref, z_ref, ab_ref, cwq_ref, cwk_ref, cwv_ref, cbq_ref, cbk_ref, cbv_ref,
                gp_ref, nw_ref, s0_ref, o_ref, sfin_ref, *scratch, chunk, precise, hp, n_h):
    hg = pl.program_id(1)
    c = pl.program_id(2)
    n_c = pl.num_programs(2)
    mm = _mm(precise)
    hist = CONV_W - 1
    base = SUBLANES - hist
    s_scrs, xp_scrs, x_scrs = scratch[:hp], scratch[hp:2 * hp], scratch[2 * hp:]
    lanes = [slice(hh * HEAD_DIM, (hh + 1) * HEAD_DIM) for hh in range(hp)]

    @pl.when(c == 0)
    def _():
        for hh in range(hp):
            s_scrs[hh][...] = s0_ref[0, hh]
            for j, cb in enumerate((cbq_ref, cbk_ref, cbv_ref)):
                xp_scrs[hh][j, base:SUBLANES, :] = cb[0, :, lanes[hh]]

    def conv(hh, j, x_r, cw_r):
        xp = xp_scrs[hh]
        xp[j, SUBLANES:SUBLANES + chunk, :] = x_r[:, lanes[hh]]
        y = xp[j, base:base + chunk, :] * cw_r[0:1, lanes[hh]]
        for t in range(1, CONV_W):
            y = y + xp[j, base + t:base + t + chunk, :] * cw_r[t:t + 1, lanes[hh]]
        xp[j, base:SUBLANES, :] = xp[j, base + chunk:SUBLANES + chunk, :]
        return _silu(y)

    ab = ab_ref[...]
    g_all = -jnp.exp(gp_ref[0:1, :]) * _softplus(ab + gp_ref[1:2, :])
    cg_all = _dot01_left(_tri(chunk).astype(BF16), g_all)
    sig_ab = jax.nn.sigmoid(ab)
    lane = lax.broadcasted_iota(jnp.int32, (1, LANES), 1)
    incl = _tri(chunk)
    strict = _tri(chunk, strict=True)
    eye = jnp.logical_and(incl, jnp.logical_not(strict))

    kept = []
    for hh in range(hp):
        h = hg * hp + hh
        qc = conv(hh, 0, q_ref, cwq_ref)
        kc = conv(hh, 1, k_ref, cwk_ref)
        vc = conv(hh, 2, v_ref, cwv_ref)
        qn = qc * lax.rsqrt(jnp.sum(qc * qc, axis=-1, keepdims=True) + EPS) * (HEAD_DIM ** -0.5)
        kn = kc * lax.rsqrt(jnp.sum(kc * kc, axis=-1, keepdims=True) + EPS)
        cg = jnp.sum(jnp.where(lane == h, cg_all, 0.0), axis=1, keepdims=True)
        beta = jnp.sum(jnp.where(lane == h + n_h, sig_ab, 0.0), axis=1, keepdims=True)
        cg_row = jnp.sum(jnp.where(eye, jnp.broadcast_to(cg, (chunk, chunk)), 0.0), axis=0, keepdims=True)
        gam = jnp.exp(jnp.where(incl, cg - cg_row, NEG_INF))
        lower = jnp.where(strict, beta * gam * mm(kn, kn, _NT), 0.0)
        s = s_scrs[hh][...]
        e_cg = jnp.exp(cg)
        x_scrs[hh][...] = beta * (vc - e_cg * mm(kn, s))
        kept.append((qn, kn, cg, gam, lower, s, e_cg))
    for j in range(chunk - 1):
        r0 = (j // SUBLANES) * SUBLANES
        for hh in range(hp):
            x_scr, lower = x_scrs[hh], kept[hh][4]
            x_scr[r0:, :] -= lower[r0:, j:j + 1] * x_scr[j:j + 1, :]
    for hh in range(hp):
        qn, kn, cg, gam, lower, s, e_cg = kept[hh]
        u = x_scrs[hh][...]
        qk = mm(qn, kn, _NT) * gam
        o = e_cg * mm(qn, s) + mm(qk, u)
        cg_end = cg[chunk - 1:chunk, :]
        s_new = jnp.exp(cg_end) * s + mm(kn * jnp.exp(cg_end - cg), u, _TN)
        s_scrs[hh][...] = s_new
        o_ref[:, lanes[hh]] = _gated_head_norm(o, nw_ref[...], z_ref[:, lanes[hh]]).astype(o_ref.dtype)

        @pl.when(c == n_c - 1)
        def _(hh=hh, s_new=s_new):
            sfin_ref[0, hh] = s_new


def _gdn(proj, proj_ab, conv_w, conv_buf, gate_params, norm_w, s0, *, seq, chunk, precise, hp):
    bsz, n_h = s0.shape[:2]
    n_c = seq // chunk
    n_hg = n_h // hp
    wide = hp * HEAD_DIM

    def col(seg):
        return pl.BlockSpec((chunk, wide), lambda b, h, c: (b * n_c + c, seg * n_hg + h))

    def cw(seg):
        return pl.BlockSpec((CONV_W, wide), lambda b, h, c: (0, seg * n_hg + h))

    def cb(seg):
        return pl.BlockSpec((1, CONV_W - 1, wide), lambda b, h, c: (b, 0, seg * n_hg + h))

    state = pl.BlockSpec((1, hp, HEAD_DIM, HEAD_DIM), lambda b, h, c: (b, h, 0, 0))
    in_specs = [col(4), col(5), col(6), col(7),
                pl.BlockSpec((chunk, LANES), lambda b, h, c: (b * n_c + c, 0)),
                cw(0), cw(1), cw(2), cb(0), cb(1), cb(2),
                pl.BlockSpec((2, LANES), lambda b, h, c: (0, 0)),
                pl.BlockSpec((1, HEAD_DIM), lambda b, h, c: (0, 0)), state]
    args = [proj, proj, proj, proj, proj_ab, conv_w, conv_w, conv_w, conv_buf, conv_buf, conv_buf,
            gate_params, norm_w.reshape(1, HEAD_DIM), s0]
    return pl.pallas_call(
        functools.partial(_gdn_kernel, chunk=chunk, precise=precise, hp=hp, n_h=n_h),
        grid=(bsz, n_hg, n_c), in_specs=in_specs,
        out_specs=(pl.BlockSpec((chunk, wide), lambda b, h, c: (b * n_c + c, h)), state),
        out_shape=(jax.ShapeDtypeStruct((bsz * seq, n_h * HEAD_DIM), F32 if precise else BF16),
                   jax.ShapeDtypeStruct(s0.shape, F32)),
        scratch_shapes=([pltpu.VMEM((HEAD_DIM, HEAD_DIM), F32)] * hp
                        + [pltpu.VMEM((3, SUBLANES + chunk, HEAD_DIM), F32)] * hp
                        + [pltpu.VMEM((chunk, HEAD_DIM), F32)] * hp),
        compiler_params=_params("parallel", "parallel", "arbitrary"), name="gdn")(*args)


def _sb_block(z, mask, run, u_incl):
    sp = _softplus(z)
    log_stay = jnp.where(mask, -sp, 0.0)
    incl = _dot01_right(log_stay, u_incl)
    later = incl - log_stay + run
    w = jnp.where(mask, jnp.exp((z - sp) + later), 0.0)
    return w, run + incl[:, 0:1]


def _sb_prompt_kernel(bias_ref, q_ref, k_ref, v_ref, o_ref, *scratch, hp):
    hg = pl.program_id(1)
    qi = pl.program_id(2)
    tq = q_ref.shape[0]
    acc_scrs, run_scrs = scratch[:hp], scratch[hp:]
    lanes = [slice(hh * HEAD_DIM, (hh + 1) * HEAD_DIM) for hh in range(hp)]
    qs = [q_ref[:, lanes[hh]].astype(BF16) for hh in range(hp)]
    biases = [bias_ref[hg * hp + hh] for hh in range(hp)]
    scale = HEAD_DIM ** -0.5
    for hh in range(hp):
        acc_scrs[hh][...] = jnp.zeros_like(acc_scrs[hh])
        run_scrs[hh][...] = jnp.zeros_like(run_scrs[hh])
    t_pos = qi * tq + lax.broadcasted_iota(jnp.int32, (tq, 1), 0)
    u_incl = _tri(SB_TK).astype(BF16)
    n_kb = (qi + 1) * (tq // SB_TK)

    def body(i, carry):
        k0 = pl.multiple_of((n_kb - 1 - i) * SB_TK, SB_TK)
        s_pos = k0 + lax.broadcasted_iota(jnp.int32, (1, SB_TK), 1)
        mask = s_pos < t_pos
        for hh in range(hp):
            kb = k_ref[pl.ds(k0, SB_TK), lanes[hh]].astype(BF16)
            vb = v_ref[pl.ds(k0, SB_TK), lanes[hh]].astype(BF16)
            z = _dot(qs[hh], kb, _NT) * scale + biases[hh]
            w, run = _sb_block(z, mask, run_scrs[hh][...], u_incl)
            acc_scrs[hh][...] += _dot(w.astype(BF16), vb)
            run_scrs[hh][...] = run
        return carry

    lax.fori_loop(0, n_kb, body, 0)
    for hh in range(hp):
        o_ref[:, lanes[hh]] = acc_scrs[hh][...].astype(o_ref.dtype)


def _sb_prompt(proj, bias, *, bsz, seq, n_h, hp, seg0):
    n_q = seq // SB_TQ
    n_hg = n_h // hp
    wide = hp * HEAD_DIM
    kv = lambda seg: pl.BlockSpec((seq, wide), lambda b, h, i, bias: (b, seg * n_hg + h))
    return pl.pallas_call(
        functools.partial(_sb_prompt_kernel, hp=hp),
        grid_spec=pltpu.PrefetchScalarGridSpec(
            num_scalar_prefetch=1, grid=(bsz, n_hg, n_q),
            in_specs=[pl.BlockSpec((SB_TQ, wide), lambda b, h, i, bias: (b * n_q + i, seg0 * n_hg + h)),
                      kv(seg0 + 1), kv(seg0 + 2)],
            out_specs=pl.BlockSpec((SB_TQ, wide), lambda b, h, i, bias: (b * n_q + i, h)),
            scratch_shapes=[pltpu.VMEM((SB_TQ, HEAD_DIM), F32)] * hp + [pltpu.VMEM((SB_TQ, 1), F32)] * hp),
        out_shape=jax.ShapeDtypeStruct((bsz * seq, n_h * HEAD_DIM), BF16),
        compiler_params=_params("parallel", "parallel", "arbitrary"), name="sb_prompt")(bias, proj, proj, proj)


def _sb_sample_kernel(pt_ref, bias_ref, q_ref, ko_ref, vo_ref, *rest, n_h, pages):
    kp_refs, vp_refs = rest[:pages], rest[pages:2 * pages]
    o_ref, acc_scr, run_scr, qall_scr, kown_scr, vown_scr = rest[2 * pages:]
    _sb_sample_body(bias_ref, q_ref, ko_ref, vo_ref, kp_refs, vp_refs, o_ref,
                    acc_scr, run_scr, qall_scr, kown_scr, vown_scr, n_h=n_h)
    del pt_ref


def _sb_sample_body(bias_ref, q_ref, ko_ref, vo_ref, kp_refs, vp_refs, o_ref,
                    acc_scr, run_scr, qall_scr, kown_scr, vown_scr, *, n_h):
    j = pl.program_id(1)
    n_j = pl.num_programs(1)
    t = q_ref.shape[0]
    n_pg = kp_refs[0].shape[0]
    t_bits = t.bit_length() - 1
    scale = HEAD_DIM ** -0.5
    lane = lax.broadcasted_iota(jnp.int32, (1, LANES), 1)
    lane_h = lax.shift_right_logical(lane, t_bits)
    lane_t = jnp.bitwise_and(lane, t - 1)
    bias_row = jnp.zeros((1, LANES), F32)
    for h in range(n_h):
        bias_row = jnp.where(lane_h == h, bias_ref[h], bias_row)

    def head_slice(h):
        return slice(h * HEAD_DIM, (h + 1) * HEAD_DIM)

    @pl.when(j == 0)
    def _():
        qall_scr[...] = jnp.zeros_like(qall_scr)
        kown_scr[...] = jnp.zeros_like(kown_scr)
        vown_scr[...] = jnp.zeros_like(vown_scr)
        for h in range(n_h):
            qall_scr[h * t:(h + 1) * t, :] = q_ref[:, head_slice(h)]
            kown_scr[h * t:(h + 1) * t, :] = ko_ref[:, head_slice(h)]
            vown_scr[h * t:(h + 1) * t, :] = vo_ref[:, head_slice(h)]
        z = _dot3(kown_scr[...], qall_scr[...], _NT) * scale + bias_row
        row = lax.broadcasted_iota(jnp.int32, (LANES, 1), 0)
        row_h = lax.shift_right_logical(row, t_bits)
        row_s = jnp.bitwise_and(row, t - 1)
        valid = jnp.logical_and(jnp.logical_and(row_h == lane_h, row_s < lane_t), lane < n_h * t)
        sp = _softplus(z)
        log_stay = jnp.where(valid, -sp, 0.0)
        later_keys =jnp.logical_and(row_h == lane_h, lane_t >= row_s)
        incl = _dot01_left(later_keys.astype(BF16), log_stay)
        w = jnp.where(valid, jnp.exp((z - sp) + (incl - log_stay)), 0.0)
        acc_scr[...] = _dot3(w.T, vown_scr[...])
        run_scr[...] = jnp.sum(log_stay, axis=0, keepdims=True)

    @pl.when(j > 0)
    def _():
        row_h = jnp.bitwise_and(lax.broadcasted_iota(jnp.int32, (n_pg * n_h, 1), 0), n_h - 1)
        valid = row_h == lane_h
        terms = []
        for kp_ref in kp_refs:
            k2 = kp_ref[...].reshape(n_pg * n_h, HEAD_DIM)
            z = _dot3(k2, qall_scr[...], _NT) * scale + bias_row
            sp = _softplus(z)
            log_stay = jnp.where(valid, -sp, 0.0)
            by_key = log_stay.reshape(n_pg, n_h, LANES)
            tail = jnp.zeros((n_h, LANES), F32)
            incl = [None] * n_pg
            for s in reversed(range(n_pg)):
                tail = tail + by_key[s]
                incl[s] = tail
            incl = jnp.stack(incl).reshape(n_pg * n_h, LANES)
            terms.append(((z - sp) + (incl - log_stay), jnp.sum(tail, axis=0, keepdims=True)))
        run = run_scr[...]
        acc = acc_scr[...]
        for (log_w, total), vp_ref in zip(terms, vp_refs):
            w = jnp.where(valid, jnp.exp(log_w + run), 0.0)
            acc = acc + _dot3(w.T, vp_ref[...].reshape(n_pg * n_h, HEAD_DIM))
            run = run + total
        acc_scr[...] = acc
        run_scr[...] = run

    @pl.when(j == n_j - 1)
    def _():
        for h in range(n_h):
            o_ref[:, head_slice(h)] = acc_scr[h * t:(h + 1) * t, :]


def _sb_sample(proj, cache_k, cache_v, layer, page_table, bias, *, bsz, seq, n_h):
    n_pages = page_table.shape[1]
    br = n_h * HEAD_DIM

    def own(seg):
        return pl.BlockSpec((seq, br), lambda b, j, pt, bias: (b, seg))

    pages = SB_PAGES_PER_STEP if n_pages % SB_PAGES_PER_STEP == 0 else 1

    def page(p):
        def index(b, j, pt, bias):
            return (layer, pt[b, n_pages - pages * (jnp.maximum(j, 1) - 1) - 1 - p], 0, 0, 0)
        return pl.BlockSpec((None, None, cache_k.shape[2], n_h, HEAD_DIM), index)

    page_specs = [page(p) for p in range(pages)]
    return pl.pallas_call(
        functools.partial(_sb_sample_kernel, n_h=n_h, pages=pages),
        grid_spec=pltpu.PrefetchScalarGridSpec(
            num_scalar_prefetch=2, grid=(bsz, n_pages // pages + 1),
            in_specs=[own(8), own(9), own(10)] + page_specs + page_specs,
            out_specs=pl.BlockSpec((seq, br), lambda b, j, pt, bias: (b, 0)),
            scratch_shapes=[pltpu.VMEM((LANES, HEAD_DIM), F32), pltpu.VMEM((1, LANES), F32)]
            + [pltpu.VMEM((LANES, HEAD_DIM), F32)] * 3),
        out_shape=jax.ShapeDtypeStruct((bsz * seq, br), F32),
        compiler_params=_params("parallel", "arbitrary"), name="sb_sample")(
            page_table, bias, proj, proj, proj, *([cache_k] * pages), *([cache_v] * pages))


def _router_kernel(x_ref, nw_ref, wr_ref, br_ref, h_ref, r_ref, *, precise):
    hn = _rms(x_ref[...], nw_ref[...])
    h_ref[...] = hn
    logits = _mm(precise)(hn, wr_ref[...]) + br_ref[...]
    lane = lax.broadcasted_iota(jnp.int32, logits.shape, 1)
    big = jnp.int32(LANES)

    def first_max(mask):
        m = jnp.max(jnp.where(mask, logits, NEG_INF), axis=-1, keepdims=True)
        idx = jnp.min(jnp.where(jnp.logical_and(mask, logits == m), lane, big), axis=-1, keepdims=True)
        return m, idx

    is_grp = lane < N_GROUPS
    g_max, grp = first_max(is_grp)
    p_grp = 1.0 / jnp.sum(jnp.where(is_grp, jnp.exp(logits - g_max), 0.0), axis=-1, keepdims=True)
    e_lo = N_GROUPS + grp * EXPERTS_PER_GROUP
    in_grp = jnp.logical_and(lane >= e_lo, lane < e_lo + EXPERTS_PER_GROUP)
    v1, i1 = first_max(in_grp)
    v2, i2 = first_max(jnp.logical_and(in_grp, lane != i1))
    e2 = jnp.exp(v2 - v1)
    w1 = (1.0 / (1.0 + e2)) * p_grp
    w2 = (e2 / (1.0 + e2)) * p_grp
    out = jnp.where(lane == 0, (i1 - N_GROUPS).astype(F32), 0.0)
    out = jnp.where(lane == 1, (i2 - N_GROUPS).astype(F32), out)
    out = jnp.where(lane == 2, w1, out)
    out = jnp.where(lane == 3, w2, out)
    r_ref[...] = out


def _router(x, norm_w, w_router, b_router, *, tm, precise):
    n, d = x.shape
    row = pl.BlockSpec((tm, d), lambda i: (i, 0))
    return pl.pallas_call(
        functools.partial(_router_kernel, precise=precise), grid=(n // tm,),
        in_specs=[row, pl.BlockSpec((1, d), lambda i: (0, 0)), pl.BlockSpec((d, LANES), lambda i: (0, 0)),
                  pl.BlockSpec((1, LANES), lambda i: (0, 0))],
        out_specs=(row, pl.BlockSpec((tm, LANES), lambda i: (i, 0))),
        out_shape=(jax.ShapeDtypeStruct((n, d), F32), jax.ShapeDtypeStruct((n, LANES), F32)),
        compiler_params=_params("parallel"), name="router")(x, norm_w.reshape(1, d), w_router, b_router)


def _new_expert(te_ref, i):
    return jnp.logical_or(i == 0, te_ref[i] != te_ref[jnp.maximum(i - 1, 0)])


def _expert_up_kernel(src_ref, te_ref, nt_ref, x_hbm, wg_ref, wu_ref, rw_ref, o_ref, xbuf, sems, wg_scr, wu_scr):
    i = pl.program_id(0)
    n_t = nt_ref[0]
    tm = o_ref.shape[0]

    def copies(step, start):
        slot = step % 2
        _row_copies(src_ref, step * tm, tm, x_hbm, xbuf.at[slot], sems.at[slot], start)

    @pl.when(i == 0)
    def _():
        copies(i, True)

    @pl.when(i + 1 < n_t)
    def _():
        copies(i + 1, True)

    @pl.when(jnp.logical_and(i < n_t, _new_expert(te_ref, i)))
    def _():
        wg_scr[...] = wg_ref[...].astype(BF16)
        wu_scr[...] = wu_ref[...].astype(BF16)

    @pl.when(i < n_t)
    def _():
        copies(i, False)
        x = xbuf[i % 2].astype(BF16)
        g = _dot(x, wg_scr[...])
        u = _dot(x, wu_scr[...])
        o_ref[...] = ((_silu(g) * u) * rw_ref[...]).astype(o_ref.dtype)

    @pl.when(i >= n_t)
    def _():
        o_ref[...] = jnp.zeros_like(o_ref)


def _expert_down_kernel(te_ref, nt_ref, h_ref, wd_ref, o_ref, wd_scr):
    i = pl.program_id(0)

    @pl.when(jnp.logical_and(i < nt_ref[0], _new_expert(te_ref, i)))
    def _():
        wd_scr[...] = wd_ref[...].astype(BF16)

    @pl.when(i < nt_ref[0])
    def _():
        o_ref[...] = _dot(h_ref[...], wd_scr[...])

    @pl.when(i >= nt_ref[0])
    def _():
        o_ref[...] = jnp.zeros_like(o_ref)


def _experts(h2, src_tok, row_w, tile_expert, n_tiles, w_gate, w_up, w_down, layer):
    r = src_tok.shape[0]
    d = h2.shape[1]
    f = w_gate.shape[-1]
    n_t = r // MOE_TILE
    hid = pl.pallas_call(
        _expert_up_kernel,
        grid_spec=pltpu.PrefetchScalarGridSpec(
            num_scalar_prefetch=3, grid=(n_t,),
            in_specs=[pl.BlockSpec(memory_space=pl.ANY),
                      pl.BlockSpec((None, None, d, f), lambda i, src, te, nt: (layer, te[i], 0, 0)),
                      pl.BlockSpec((None, None, d, f), lambda i, src, te, nt: (layer, te[i], 0, 0)),
                      pl.BlockSpec((MOE_TILE, 1), lambda i, src, te, nt: (i, 0))],
            out_specs=pl.BlockSpec((MOE_TILE, f), lambda i, src, te, nt: (i, 0)),
            scratch_shapes=[pltpu.VMEM((2, MOE_TILE, d), F32), pltpu.SemaphoreType.DMA((2,)),
                            pltpu.VMEM((d, f), BF16), pltpu.VMEM((d, f), BF16)]),
        out_shape=jax.ShapeDtypeStruct((r, f), BF16),
        compiler_params=_params("arbitrary"), name="expert_up")(
            src_tok, tile_expert, n_tiles, h2, w_gate, w_up, row_w)
    return pl.pallas_call(
        _expert_down_kernel,
        grid_spec=pltpu.PrefetchScalarGridSpec(
            num_scalar_prefetch=2, grid=(n_t,),
            in_specs=[pl.BlockSpec((MOE_TILE, f), lambda i, te, nt: (i, 0)),
                      pl.BlockSpec((None, None, f, d), lambda i, te, nt: (layer, te[i], 0, 0))],
            out_specs=pl.BlockSpec((MOE_TILE, d), lambda i, te, nt: (i, 0)),
            scratch_shapes=[pltpu.VMEM((f, d), BF16)]),
        out_shape=jax.ShapeDtypeStruct((r, d), F32),
        compiler_params=_params("arbitrary"), name="expert_down")(tile_expert, n_tiles, hid, w_down)


def _moe_schedule(eid, wsel):
    n = eid.shape[0]
    flat_e = eid.reshape(-1)
    onehot = (flat_e[:, None] == jnp.arange(N_EXPERTS, dtype=jnp.int32)[None, :]).astype(jnp.int32)
    rank = jnp.sum((jnp.cumsum(onehot, axis=0) - onehot) * onehot, axis=1)
    counts = jnp.sum(onehot, axis=0)
    tiles_e = (counts + MOE_TILE - 1) // MOE_TILE
    tiles_end = jnp.cumsum(tiles_e)
    row_start = (tiles_end - tiles_e) * MOE_TILE
    dest = row_start[flat_e] + rank
    n_tiles_max = (2 * n + MOE_TILE - 1) // MOE_TILE + N_EXPERTS
    n_rows = n_tiles_max * MOE_TILE
    src_tok = jnp.zeros((n_rows,), jnp.int32).at[dest].set(jnp.arange(2 * n, dtype=jnp.int32) // 2)
    row_w = jnp.zeros((n_rows,), F32).at[dest].set(wsel.reshape(-1))
    n_tiles = tiles_end[-1]
    tile_ids = jnp.minimum(jnp.arange(n_tiles_max, dtype=jnp.int32), n_tiles - 1)
    tile_expert = jnp.sum((tile_ids[:, None] >= tiles_end[None, :]).astype(jnp.int32), axis=1)
    return dest, src_tok, row_w.reshape(n_rows, 1), tile_expert, n_tiles.reshape(1).astype(jnp.int32)


def _moe_sorted(x, norm_w, w_router, b_router, w_gate, w_up, w_down, layer):
    n = x.shape[0]
    h2, routed = _router(x, norm_w, w_router, b_router, tm=NORM_TILE, precise=False)
    eid = routed[:, 0:2].astype(jnp.int32)
    wsel = routed[:, 2:4]
    dest, src_tok, row_w, tile_expert, n_tiles = _moe_schedule(eid, wsel)
    y_rows = _experts(h2, src_tok, row_w, tile_expert, n_tiles, w_gate, w_up, w_down, layer)
    back = dest.reshape(n, 2).T.reshape(-1)
    return y_rows, back


def _experts_dense_kernel(x_ref, cw_ref, wg_ref, wu_ref, wd_ref, o_ref):
    e = pl.program_id(0)
    fi = pl.program_id(1)

    @pl.when(jnp.logical_and(e == 0, fi == 0))
    def _():
        o_ref[...] = jnp.zeros_like(o_ref)

    x = x_ref[...]
    lane = lax.broadcasted_iota(jnp.int32, (1, LANES), 1)
    cw = jnp.sum(jnp.where(lane == e, cw_ref[...], 0.0), axis=1, keepdims=True)
    hid = (_silu(_dot3(x, wg_ref[...])) * _dot3(x, wu_ref[...])) * cw
    o_ref[...] += _dot3(hid, wd_ref[...])


def _moe_dense(x, norm_w, w_router, b_router, w_gate, w_up, w_down, layer):
    n, d = x.shape
    f = w_gate.shape[-1]
    tf = f // 2
    h2, routed = _router(x, norm_w, w_router, b_router, tm=n, precise=True)
    eid = routed[:, 0:2].astype(jnp.int32)
    combine = jnp.sum(jax.nn.one_hot(eid, LANES, dtype=F32) * routed[:, 2:4, None], axis=1)
    full = lambda shape: pl.BlockSpec(shape, lambda e, fi: (0, 0))
    return pl.pallas_call(
        _experts_dense_kernel, grid=(N_EXPERTS, f // tf),
        in_specs=[full((n, d)), full((n, LANES)),
                  pl.BlockSpec((None, None, d, tf), lambda e, fi: (layer, e, 0, fi)),
                  pl.BlockSpec((None, None, d, tf), lambda e, fi: (layer, e, 0, fi)),
                  pl.BlockSpec((None, None, tf, d), lambda e, fi: (layer, e, fi, 0))],
        out_specs=full((n, d)), out_shape=jax.ShapeDtypeStruct((n, d), F32),
        compiler_params=_params("arbitrary", "arbitrary"), name="experts_dense")(
            h2, combine, w_gate, w_up, w_down)


def kernel(x_prompt, x_sample, state_hgrn, state_gdn, state_gdn_conv, cache_k, cache_v, page_table, norm_mix, w_in, hgrn_lb, hgrn_norm, gdn_conv, gdn_a_log, gdn_dt_bias, gdn_norm, sb_bias, w_branch, w_out, norm_ffn, w_router_group, b_router_group, w_router_expert, b_router_expert, w_exp_gate, w_exp_up, w_exp_down, final_norm):
    bp, tp, d = x_prompt.shape
    bs, ts, _ = x_sample.shape
    depth = w_in.shape[0]
    n_h = state_hgrn.shape[2]
    br = n_h * HEAD_DIM
    n_p, n_s = bp * tp, bs * ts
    assert n_p % IN_TILE == 0 and n_p % ROW_TILE == 0 and n_p % NORM_TILE == 0
    assert tp % CHUNK == 0 and tp % SB_TQ == 0 and ts % SUBLANES == 0 and n_s % SUBLANES == 0
    assert 2 * n_h <= LANES and br % PROJ_TN == 0
    assert n_h * ts <= LANES and ts & (ts - 1) == 0 and n_h & (n_h - 1) == 0

    heads_per_step = min(HEADS_PER_STEP, n_h)
    ab0 = 8 * br
    ab1 = ab0 + 2 * n_h
    n_main = 11 * br + 3 * d
    gate_col0 = 11 * br
    w_in_t = jnp.swapaxes(w_in, 1, 2)
    lb_all = jnp.cumsum(jax.nn.softmax(hgrn_lb.astype(F32), axis=0), axis=0)
    zeros_h = jnp.zeros((bp, n_h, HEAD_DIM, HEAD_DIM), F32)
    zeros_conv = jnp.zeros((bp, CONV_W - 1, 3 * br), F32)

    xp = x_prompt.reshape(n_p, d)
    xs = x_sample.reshape(n_s, d)
    outs_p = [[] for _ in range(5)]
    outs_s = [[] for _ in range(5)]
    hp = _rmsnorm(xp, norm_mix[0], BF16, ROW_TILE)
    hs = _rmsnorm(xs, norm_mix[0], F32, n_s)
    for l in range(depth):
        lb = (lb_all[l] - lb_all[0]).reshape(1, br)
        lb_params = (jnp.log(lb), jnp.log1p(-lb), 1.0 - lb)
        gate_params = jnp.pad(jnp.stack([gdn_a_log[l], gdn_dt_bias[l]]).astype(F32), ((0, 0), (0, LANES - n_h)))
        conv_w = gdn_conv[l].astype(F32)
        bias = sb_bias[l].astype(F32)
        w_router = jnp.pad(jnp.concatenate([w_router_group[l], w_router_expert[l]], axis=1),
                           ((0, 0), (0, LANES - N_GROUPS - N_EXPERTS)))
        b_router = jnp.pad(jnp.concatenate([b_router_group[l], b_router_expert[l]]).astype(F32),
                           (0, LANES - N_GROUPS - N_EXPERTS)).reshape(1, LANES)
        next_norm = norm_mix[l + 1] if l + 1 < depth else final_norm

        proj_ab_cols = _matmul_t(hp, w_in_t, l, tm=IN_TILE, tn=PROJ_TN, row0=0, n_cols=ab0, precise=False,
                                 name="in_proj_ab")
        proj_c_cols = _matmul_t(hp, w_in_t, l, tm=IN_TILE, tn=PROJ_TN, row0=ab1, n_cols=n_main - ab0,
                                precise=False, name="in_proj_c")
        proj_ab = _matmul_t(hp, w_in_t, l, tm=IN_TILE, tn=LANES, row0=ab0, n_cols=LANES, precise=False,
                            name="in_proj_gates")
        o_a, hg_p = _hgrn(proj_ab_cols, lb_params, hgrn_norm[l], zeros_h, seq=tp, chunk=CHUNK, precise=False,
                          hp=heads_per_step)
        o_b, gd_p = _gdn(proj_ab_cols, proj_ab, conv_w, zeros_conv, gate_params, gdn_norm[l], zeros_h,
                         seq=tp, chunk=CHUNK, precise=False, hp=heads_per_step)
        o_c = _sb_prompt(proj_c_cols, bias, bsz=bp, seq=tp, n_h=n_h, hp=min(SB_HEADS_PER_STEP, n_h), seg0=0)
        mixed = _merge(o_a, o_b, o_c, w_branch, l, proj_c_cols, gate_col0 - ab0, d, tm=ROW_TILE, precise=False)
        xp = _matmul(mixed, w_out, layer=l, tm=ROW_TILE, tn=PROJ_TN, residual=xp, weight_resident=True,
                     name="out_proj")
        y_rows, back = _moe_sorted(xp, norm_ffn[l], w_router, b_router, w_exp_gate, w_exp_up, w_exp_down, l)
        xp, hp = _moe_add_norm(xp, y_rows, back, next_norm, BF16 if l + 1 < depth else F32)
        outs_p[0].append(hg_p)
        outs_p[1].append(gd_p)
        outs_p[2].append(proj_ab_cols[:, 4 * br:7 * br].reshape(bp, tp, 3 * br)[:, tp - (CONV_W - 1):])
        outs_p[3].append(proj_c_cols[:, br:2 * br].reshape(bp, tp, n_h, HEAD_DIM))
        outs_p[4].append(proj_c_cols[:, 2 * br:3 * br].reshape(bp, tp, n_h, HEAD_DIM))

        proj_s = jnp.concatenate([
            _matmul_t(hs, w_in_t, l, tm=n_s, tn=PROJ_TN, row0=0, n_cols=ab0, precise=True, name="in_proj_s0"),
            _matmul_t(hs, w_in_t, l, tm=n_s, tn=PROJ_TN, row0=ab1, n_cols=n_main - ab0, precise=True,
                      name="in_proj_s1")], axis=1)
        proj_ab_s = _matmul_t(hs, w_in_t, l, tm=n_s, tn=LANES, row0=ab0, n_cols=LANES, precise=True,
                              name="in_proj_gates_s")
        oa_s, hg_s = _hgrn(proj_s, lb_params, hgrn_norm[l], state_hgrn[l].astype(F32), seq=ts, chunk=ts,
                           precise=True, hp=heads_per_step)
        ob_s, gd_s = _gdn(proj_s, proj_ab_s, conv_w, state_gdn_conv[l].astype(F32), gate_params, gdn_norm[l],
                          state_gdn[l].astype(F32), seq=ts, chunk=ts, precise=True, hp=heads_per_step)
        oc_s = _sb_sample(proj_s, cache_k, cache_v, l, page_table, bias, bsz=bs, seq=ts, n_h=n_h)
        mixed_s = _merge(oa_s, ob_s, oc_s, w_branch, l, proj_s, gate_col0, d, tm=n_s, precise=True)
        xs = _matmul_precise(mixed_s, w_out, layer=l, tn=PROJ_TN, residual=xs, name="out_proj_s")
        y_s = _moe_dense(xs, norm_ffn[l], w_router, b_router, w_exp_gate, w_exp_up, w_exp_down, l)
        xs, hs = _add_norm(xs, y_s, next_norm, F32)
        outs_s[0].append(hg_s)
        outs_s[1].append(gd_s)
        outs_s[2].append(proj_s[:, 4 * br:7 * br].reshape(bs, ts, 3 * br)[:, ts - (CONV_W - 1):])
        outs_s[3].append(proj_s[:, 9 * br:10 * br].reshape(bs, ts, n_h, HEAD_DIM))
        outs_s[4].append(proj_s[:, 10 * br:11 * br].reshape(bs, ts, n_h, HEAD_DIM))

    y_prompt = hp.reshape(bp, tp, d)
    y_sample = hs.reshape(bs, ts, d)
    return (y_prompt, y_sample, *(jnp.stack(o) for o in outs_p), *(jnp.stack(o) for o in outs_s))
```

```python
import functools

import jax
import jax.numpy as jnp
from jax import lax
from jax.experimental import pallas as pl
from jax.experimental.pallas import tpu as pltpu

F32 = jnp.float32
BF16 = jnp.bfloat16

HEAD_DIM = 128
CONV_W = 4
CHUNK = 64
N_GROUPS = 4
EXPERTS_PER_GROUP = 8
N_EXPERTS = N_GROUPS * EXPERTS_PER_GROUP
EPS = 1e-6
LANES = 128
SUBLANES = 8
VMEM_LIMIT_BYTES = 56 * 1024 * 1024
NEG_INF = float("-inf")

IN_TILE = 1024
ROW_TILE = 512
NORM_TILE = 256
PROJ_TN = 512
MOE_TILE = 256
EXPERT_K_CHUNKS = 8
SB_TQ = 512
SB_TK = 128
HEADS_PER_STEP = 8
SB_PAGES_PER_STEP = 4
SB_HEADS_PER_STEP = 8


def _params(*sem):
    return pltpu.CompilerParams(dimension_semantics=sem, vmem_limit_bytes=VMEM_LIMIT_BYTES)


def _split_bf16(x, n):
    parts = []
    r = x
    for i in range(n):
        p = r.astype(BF16)
        parts.append(p)
        if i + 1 < n:
            r = r - p.astype(F32)
    return parts


_NN = (((1,), (0,)), ((), ()))
_NT = (((1,), (1,)), ((), ()))
_TN = (((0,), (0,)), ((), ()))


def _dot(a, b, dims=_NN):
    return lax.dot_general(a, b, dims, preferred_element_type=F32)


def _dotb(a, b, dims=_NN):
    return _dot(a.astype(BF16), b.astype(BF16), dims)


def _dot3(a, b, dims=_NN):
    a_hi, a_lo = _split_bf16(a, 2)
    b_hi, b_lo = _split_bf16(b, 2)
    return _dot(a_hi, b_hi, dims) + (_dot(a_hi, b_lo, dims) + _dot(a_lo, b_hi, dims))


def _mm(precise):
    return _dot3 if precise else _dotb


def _dot01_left(m01, x):
    p0, p1, p2 = _split_bf16(x, 3)
    return _dot(m01, p0) + (_dot(m01, p1) + _dot(m01, p2))


def _dot01_right(x, m01):
    p0, p1, p2 = _split_bf16(x, 3)
    return _dot(p0, m01) + (_dot(p1, m01) + _dot(p2, m01))


def _softplus(x):
    return jnp.maximum(x, 0.0) + jnp.log1p(jnp.exp(-jnp.abs(x)))


def _silu(x):
    return x * jax.nn.sigmoid(x)


def _tri(n, strict=False):
    r = lax.broadcasted_iota(jnp.int32, (n, n), 0)
    c = lax.broadcasted_iota(jnp.int32, (n, n), 1)
    return (r > c) if strict else (r >= c)


def _rms(x, w):
    return x * lax.rsqrt(jnp.mean(x * x, axis=-1, keepdims=True) + EPS) * w


def _gated_head_norm(o, w, z):
    return _rms(o, w) * _silu(z)


def _rmsnorm_kernel(x_ref, w_ref, o_ref):
    o_ref[...] = _rms(x_ref[...], w_ref[...]).astype(o_ref.dtype)


def _rmsnorm(x, w, out_dtype, tm):
    n, d = x.shape
    row = pl.BlockSpec((tm, d), lambda i: (i, 0))
    return pl.pallas_call(
        _rmsnorm_kernel, grid=(n // tm,),
        in_specs=[row, pl.BlockSpec((1, d), lambda i: (0, 0))],
        out_specs=row, out_shape=jax.ShapeDtypeStruct((n, d), out_dtype),
        compiler_params=_params("parallel"), name="rmsnorm")(x, w.reshape(1, d))


def _add_norm_kernel(x_ref, y_ref, w_ref, xo_ref, ho_ref):
    x = x_ref[...] + y_ref[...]
    xo_ref[...] = x
    ho_ref[...] = _rms(x, w_ref[...]).astype(ho_ref.dtype)


def _add_norm(x, y, w, out_dtype):
    n, d = x.shape
    return pl.pallas_call(
        _add_norm_kernel,
        out_shape=(jax.ShapeDtypeStruct((n, d), F32), jax.ShapeDtypeStruct((n, d), out_dtype)),
        compiler_params=pltpu.CompilerParams(vmem_limit_bytes=VMEM_LIMIT_BYTES),
        name="add_norm")(x, y, w.reshape(1, d))


def _row_copies(idx_ref, idx0, n_rows, src_hbm, dst_vmem, sem, start):
    def body(r, carry):
        cp = pltpu.make_async_copy(src_hbm.at[pl.ds(idx_ref[idx0 + r], 1), :], dst_vmem.at[pl.ds(r, 1), :], sem)
        if start:
            cp.start()
        else:
            cp.wait()
        return carry

    lax.fori_loop(0, n_rows, body, 0, unroll=8)


def _moe_add_norm_kernel(back_ref, x_ref, y_hbm, w_ref, xo_ref, ho_ref, ybuf, sems):
    i = pl.program_id(0)
    n_i = pl.num_programs(0)
    tm = x_ref.shape[0]
    n_tok = n_i * tm

    def copies(step, start):
        slot = step % 2
        for k in range(2):
            _row_copies(back_ref, k * n_tok + step * tm, tm, y_hbm, ybuf.at[slot, k], sems.at[slot], start)

    @pl.when(i == 0)
    def _():
        copies(i, True)

    @pl.when(i + 1 < n_i)
    def _():
        copies(i + 1, True)

    copies(i, False)
    slot = i % 2
    x = x_ref[...] + (ybuf[slot, 0] + ybuf[slot, 1])
    xo_ref[...] = x
    ho_ref[...] = _rms(x, w_ref[...]).astype(ho_ref.dtype)


def _moe_add_norm(x, y_rows, back, w, out_dtype):
    n, d = x.shape
    tm = NORM_TILE
    row = pl.BlockSpec((tm, d), lambda i, back: (i, 0))
    return pl.pallas_call(
        _moe_add_norm_kernel,
        grid_spec=pltpu.PrefetchScalarGridSpec(
            num_scalar_prefetch=1, grid=(n // tm,),
            in_specs=[row, pl.BlockSpec(memory_space=pl.ANY), pl.BlockSpec((1, d), lambda i, back: (0, 0))],
            out_specs=(row, row),
            scratch_shapes=[pltpu.VMEM((2, 2, tm, d), F32), pltpu.SemaphoreType.DMA((2,))]),
        out_shape=(jax.ShapeDtypeStruct((n, d), F32), jax.ShapeDtypeStruct((n, d), out_dtype)),
        compiler_params=_params("arbitrary"), name="moe_add_norm")(back, x, y_rows, w.reshape(1, d))


def _mm_kernel(a_ref, b_ref, *rest, has_res):
    o_ref = rest[-1]
    out = _dotb(a_ref[...], b_ref[...])
    if has_res:
        out = out + rest[0][...]
    o_ref[...] = out.astype(o_ref.dtype)


def _mm_wres_kernel(a_ref, b_ref, *rest, has_res):
    o_ref, w_scr = rest[-2], rest[-1]

    @pl.when(pl.program_id(1) == 0)
    def _():
        w_scr[...] = b_ref[...].astype(BF16)

    out = _dot(a_ref[...].astype(BF16), w_scr[...])
    if has_res:
        out = out + rest[0][...]
    o_ref[...] = out.astype(o_ref.dtype)


def _weight_spec(b, layer, k, tn, col_block):
    if b.ndim == 2:
        return pl.BlockSpec((k, tn), lambda *g: (0, col_block(*g)))
    return pl.BlockSpec((None, k, tn), lambda *g: (layer, 0, col_block(*g)))


def _matmul(a, b, *, tm, tn, layer=None, col0=0, n_cols=None, residual=None, weight_resident=False,
            name="matmul"):
    m, k = a.shape
    n_cols = b.shape[-1] if n_cols is None else n_cols
    blk0 = col0 // tn
    if weight_resident:
        grid, row, colb = (n_cols // tn, m // tm), (lambda j, i: i), (lambda j, i: j)
        body, scratch, sem = _mm_wres_kernel, [pltpu.VMEM((k, tn), BF16)], ("parallel", "arbitrary")
    else:
        grid, row, colb = (m // tm, n_cols // tn), (lambda i, j: i), (lambda i, j: j)
        body, scratch, sem = _mm_kernel, [], ("parallel", "parallel")
    in_specs = [pl.BlockSpec((tm, k), lambda *g: (row(*g), 0)),
                _weight_spec(b, layer, k, tn, lambda *g: blk0 + colb(*g))]
    args = [a, b]
    if residual is not None:
        in_specs.append(pl.BlockSpec((tm, tn), lambda *g: (row(*g), colb(*g))))
        args.append(residual)
    return pl.pallas_call(
        functools.partial(body, has_res=residual is not None),
        grid=grid, in_specs=in_specs,
        out_specs=pl.BlockSpec((tm, tn), lambda *g: (row(*g), colb(*g))),
        out_shape=jax.ShapeDtypeStruct((m, n_cols), F32), scratch_shapes=scratch,
        compiler_params=_params(*sem), name=name)(*args)


def _mm_t_kernel(a_ref, bt_ref, o_ref, *, precise):
    o_ref[...] = _mm(precise)(a_ref[...], bt_ref[0], _NT)


def _matmul_t(a, bt, layer, *, tm, tn, row0, n_cols, precise, name):
    m, k = a.shape
    bt_spec = pl.BlockSpec((pl.Element(1), pl.Element(tn), pl.Element(k)),
                           lambda i, j: (layer, pl.multiple_of(row0 + j * tn, SUBLANES), 0))
    return pl.pallas_call(
        functools.partial(_mm_t_kernel, precise=precise), grid=(m // tm, n_cols // tn),
        in_specs=[pl.BlockSpec((tm, k), lambda i, j: (i, 0)), bt_spec],
        out_specs=pl.BlockSpec((tm, tn), lambda i, j: (i, j)),
        out_shape=jax.ShapeDtypeStruct((m, n_cols), F32),
        compiler_params=_params("parallel", "parallel"), name=name)(a, bt)


def _mm_precise_kernel(a_ref, b_ref, *rest, presplit, has_res):
    o_ref = rest[-1]
    if presplit:
        a_hi, a_lo = _split_bf16(a_ref[...], 2)
        b_hi, b_lo = b_ref[...], rest[0][...]
        out = _dot(a_hi, b_hi) + (_dot(a_hi, b_lo) + _dot(a_lo, b_hi))
    else:
        out = _dot3(a_ref[...], b_ref[...])
    if has_res:
        out = out + rest[-2][...]
    o_ref[...] = out


def _matmul_precise(a, b, *, tn, b_lo=None, layer=None, residual=None, name="matmul_precise"):
    m, k = a.shape
    n_cols = b.shape[-1]
    in_specs = [pl.BlockSpec((m, k), lambda j: (0, 0)), _weight_spec(b, layer, k, tn, lambda j: j)]
    args = [a, b]
    if b_lo is not None:
        in_specs.append(_weight_spec(b_lo, layer, k, tn, lambda j: j))
        args.append(b_lo)
    if residual is not None:
        in_specs.append(pl.BlockSpec((m, tn), lambda j: (0, j)))
        args.append(residual)
    return pl.pallas_call(
        functools.partial(_mm_precise_kernel, presplit=b_lo is not None, has_res=residual is not None),
        grid=(n_cols // tn,), in_specs=in_specs,
        out_specs=pl.BlockSpec((m, tn), lambda j: (0, j)),
        out_shape=jax.ShapeDtypeStruct((m, n_cols), F32),
        compiler_params=_params("parallel"), name=name)(*args)


def _merge_kernel(oa_ref, ob_ref, oc_ref, wb_ref, ga_ref, gb_ref, gc_ref, o_ref, *scratch, precise):
    if precise:
        weight = lambda j: wb_ref[j]
    else:
        w_scr, = scratch

        @pl.when(pl.program_id(1) == 0)
        def _():
            w_scr[...] = wb_ref[...].astype(BF16)

        weight = lambda j: w_scr[j]
    acc = None
    for j, (o_r, g_r) in enumerate(((oa_ref, ga_ref), (ob_ref, gb_ref), (oc_ref, gc_ref))):
        term = jax.nn.sigmoid(g_r[...]) * _mm(precise)(o_r[...], weight(j))
        acc = term if acc is None else acc + term
    o_ref[...] = acc.astype(o_ref.dtype)


def _merge(o_a, o_b, o_c, w_branch, layer, proj, gate_col0, d_model, *, tm, precise):
    n, br = o_a.shape
    tn = PROJ_TN
    g0 = gate_col0 // tn
    gstep = d_model // tn
    o_spec = pl.BlockSpec((tm, br), lambda j, i: (i, 0))
    g_specs = [pl.BlockSpec((tm, tn), functools.partial(lambda j, i, b: (i, g0 + b * gstep + j), b=b))
               for b in range(3)]
    return pl.pallas_call(
        functools.partial(_merge_kernel, precise=precise), grid=(d_model // tn, n // tm),
        in_specs=[o_spec, o_spec, o_spec, pl.BlockSpec((None, 3, br, tn), lambda j, i: (layer, 0, 0, j))] + g_specs,
        out_specs=pl.BlockSpec((tm, tn), lambda j, i: (i, j)),
        out_shape=jax.ShapeDtypeStruct((n, d_model), F32 if precise else BF16),
        scratch_shapes=[] if precise else [pltpu.VMEM((3, br, tn), BF16)],
        compiler_params=_params("parallel", "arbitrary"), name="merge")(o_a, o_b, o_c, w_branch, proj, proj, proj)


def _hgrn_kernel(q_ref, f_ref, i_ref, g_ref, llb_ref, l1m_ref, oml_ref, nw_ref, s0_ref,
                 o_ref, sfin_ref, *scratch, chunk, precise, hp):
    c = pl.program_id(2)
    n_c = pl.num_programs(2)
    mm = _mm(precise)
    operand = (lambda x: x) if precise else (lambda x: x.astype(BF16).astype(F32))
    st_scrs, o_scrs, qkvb_scrs = scratch[:hp], scratch[hp:2 * hp], scratch[2 * hp:]
    lanes = [slice(hh * HEAD_DIM, (hh + 1) * HEAD_DIM) for hh in range(hp)]

    @pl.when(c == 0)
    def _():
        for hh in range(hp):
            st_scrs[hh][...] = s0_ref[0, hh].T

    kept = []
    for hh in range(hp):
        af = f_ref[:, lanes[hh]]
        log_sig = jnp.minimum(af, 0.0) - jnp.log1p(jnp.exp(-jnp.abs(af)))
        a = llb_ref[:, lanes[hh]]
        b2 = l1m_ref[:, lanes[hh]] + log_sig
        log_f = jnp.maximum(a, b2) + jnp.log1p(jnp.exp(-jnp.abs(a - b2)))
        k = oml_ref[:, lanes[hh]] * jax.nn.sigmoid(-af)
        q = _silu(q_ref[:, lanes[hh]])
        v = i_ref[:, lanes[hh]]
        b = _dot01_left(_tri(chunk).astype(BF16), log_f)
        st = st_scrs[hh][...]
        o_scrs[hh][...] = mm(q * jnp.exp(b), st, _NT)
        qkvb_scrs[hh][0] = q
        qkvb_scrs[hh][1] = k
        qkvb_scrs[hh][2] = operand(v)
        qkvb_scrs[hh][3] = b
        kept.append((k, v, b, st))
    for s in range(chunk):
        r0 = (s // SUBLANES) * SUBLANES
        rows = r0 + lax.broadcasted_iota(jnp.int32, (chunk - r0, 1), 0)
        for hh in range(hp):
            qkvb, o_scr = qkvb_scrs[hh], o_scrs[hh]
            d = jnp.where(rows >= s, qkvb[3, r0:, :] - qkvb[3, s:s + 1, :], NEG_INF)
            col = jnp.sum(qkvb[0, r0:, :] * qkvb[1, s:s + 1, :] * jnp.exp(d), axis=-1, keepdims=True)
            o_scr[r0:, :] += operand(col) * qkvb[2, s:s + 1, :]
    for hh in range(hp):
        k, v, b, st = kept[hh]
        o_ref[:, lanes[hh]] = _gated_head_norm(o_scrs[hh][...], nw_ref[...], g_ref[:, lanes[hh]]).astype(o_ref.dtype)
        b_end = b[chunk - 1:chunk, :]
        st_new = st * jnp.exp(b_end) + mm(v, k * jnp.exp(b_end - b), _TN)
        st_scrs[hh][...] = st_new

        @pl.when(c == n_c - 1)
        def _(hh=hh, st_new=st_new):
            sfin_ref[0, hh] = st_new.T


def _hgrn(proj, lb_params, norm_w, s0, *, seq, chunk, precise, hp):
    bsz, n_h = s0.shape[:2]
    n_c = seq // chunk
    n_hg = n_h // hp
    wide = hp * HEAD_DIM

    def col(seg):
        return pl.BlockSpec((chunk, wide), lambda b, h, c: (b * n_c + c, seg * n_hg + h))

    par = pl.BlockSpec((1, wide), lambda b, h, c: (0, h))
    state = pl.BlockSpec((1, hp, HEAD_DIM, HEAD_DIM), lambda b, h, c: (b, h, 0, 0))
    in_specs = [col(0), col(1), col(2), col(3), par, par, par,
                pl.BlockSpec((1, HEAD_DIM), lambda b, h, c: (0, 0)), state]
    args = [proj, proj, proj, proj, *lb_params, norm_w.reshape(1, HEAD_DIM), s0]
    return pl.pallas_call(
        functools.partial(_hgrn_kernel, chunk=chunk, precise=precise, hp=hp),
        grid=(bsz, n_hg, n_c), in_specs=in_specs,
        out_specs=(pl.BlockSpec((chunk, wide), lambda b, h, c: (b * n_c + c, h)), state),
        out_shape=(jax.ShapeDtypeStruct((bsz * seq, n_h * HEAD_DIM), F32 if precise else BF16),
                   jax.ShapeDtypeStruct(s0.shape, F32)),
        scratch_shapes=([pltpu.VMEM((HEAD_DIM, HEAD_DIM), F32)] * hp + [pltpu.VMEM((chunk, HEAD_DIM), F32)] * hp
                        + [pltpu.VMEM((4, chunk, HEAD_DIM), F32)] * hp),
        compiler_params=_params("parallel", "parallel", "arbitrary"), name="hgrn")(*args)


def _gdn_kernel(q_ref, k_ref, v_ref, z_ref, ab_ref, cwq_ref, cwk_ref, cwv_ref, cbq_ref, cbk_ref, cbv_ref,
                gp_ref, nw_ref, s0_ref, o_ref, sfin_ref, *scratch, chunk, precise, hp, n_h):
    hg = pl.program_id(1)
    c = pl.program_id(2)
    n_c = pl.num_programs(2)
    mm = _mm(precise)
    hist = CONV_W - 1
    base = SUBLANES - hist
    s_scrs, xp_scrs, x_scrs = scratch[:hp], scratch[hp:2 * hp], scratch[2 * hp:]
    lanes = [slice(hh * HEAD_DIM, (hh + 1) * HEAD_DIM) for hh in range(hp)]

    @pl.when(c == 0)
    def _():
        for hh in range(hp):
            s_scrs[hh][...] = s0_ref[0, hh]
            for j, cb in enumerate((cbq_ref, cbk_ref, cbv_ref)):
                xp_scrs[hh][j, base:SUBLANES, :] = cb[0, :, lanes[hh]]

    def conv(hh, j, x_r, cw_r):
        xp = xp_scrs[hh]
        xp[j, SUBLANES:SUBLANES + chunk, :] = x_r[:, lanes[hh]]
        y = xp[j, base:base + chunk, :] * cw_r[0:1, lanes[hh]]
        for t in range(1, CONV_W):
            y = y + xp[j, base + t:base + t + chunk, :] * cw_r[t:t + 1, lanes[hh]]
        xp[j, base:SUBLANES, :] = xp[j, base + chunk:SUBLANES + chunk, :]
        return _silu(y)

    ab = ab_ref[...]
    g_all = -jnp.exp(gp_ref[0:1, :]) * _softplus(ab + gp_ref[1:2, :])
    cg_all = _dot01_left(_tri(chunk).astype(BF16), g_all)
    sig_ab = jax.nn.sigmoid(ab)
    lane = lax.broadcasted_iota(jnp.int32, (1, LANES), 1)
    incl = _tri(chunk)
    strict = _tri(chunk, strict=True)
    eye = jnp.logical_and(incl, jnp.logical_not(strict))

    kept = []
    for hh in range(hp):
        h = hg * hp + hh
        qc = conv(hh, 0, q_ref, cwq_ref)
        kc = conv(hh, 1, k_ref, cwk_ref)
        vc = conv(hh, 2, v_ref, cwv_ref)
        qn = qc * lax.rsqrt(jnp.sum(qc * qc, axis=-1, keepdims=True) + EPS) * (HEAD_DIM ** -0.5)
        kn = kc * lax.rsqrt(jnp.sum(kc * kc, axis=-1, keepdims=True) + EPS)
        cg = jnp.sum(jnp.where(lane == h, cg_all, 0.0), axis=1, keepdims=True)
        beta = jnp.sum(jnp.where(lane == h + n_h, sig_ab, 0.0), axis=1, keepdims=True)
        cg_row = jnp.sum(jnp.where(eye, jnp.broadcast_to(cg, (chunk, chunk)), 0.0), axis=0, keepdims=True)
        gam = jnp.exp(jnp.where(incl, cg - cg_row, NEG_INF))
        lower = jnp.where(strict, beta * gam * mm(kn, kn, _NT), 0.0)
        s = s_scrs[hh][...]
        e_cg = jnp.exp(cg)
        x_scrs[hh][...] = beta * (vc - e_cg * mm(kn, s))
        kept.append((qn, kn, cg, gam, lower, s, e_cg))
    for j in range(chunk - 1):
        r0 = (j // SUBLANES) * SUBLANES
        for hh in range(hp):
            x_scr, lower = x_scrs[hh], kept[hh][4]
            x_scr[r0:, :] -= lower[r0:, j:j + 1] * x_scr[j:j + 1, :]
    for hh in range(hp):
        qn, kn, cg, gam, lower, s, e_cg = kept[hh]
        u = x_scrs[hh][...]
        qk = mm(qn, kn, _NT) * gam
        o = e_cg * mm(qn, s) + mm(qk, u)
        cg_end = cg[chunk - 1:chunk, :]
        s_new = jnp.exp(cg_end) * s + mm(kn * jnp.exp(cg_end - cg), u, _TN)
        s_scrs[hh][...] = s_new
        o_ref[:, lanes[hh]] = _gated_head_norm(o, nw_ref[...], z_ref[:, lanes[hh]]).astype(o_ref.dtype)

        @pl.when(c == n_c - 1)
        def _(hh=hh, s_new=s_new):
            sfin_ref[0, hh] = s_new


def _gdn(proj, proj_ab, conv_w, conv_buf, gate_params, norm_w, s0, *, seq, chunk, precise, hp):
    bsz, n_h = s0.shape[:2]
    n_c = seq // chunk
    n_hg = n_h // hp
    wide = hp * HEAD_DIM

    def col(seg):
        return pl.BlockSpec((chunk, wide), lambda b, h, c: (b * n_c + c, seg * n_hg + h))

    def cw(seg):
        return pl.BlockSpec((CONV_W, wide), lambda b, h, c: (0, seg * n_hg + h))

    def cb(seg):
        return pl.BlockSpec((1, CONV_W - 1, wide), lambda b, h, c: (b, 0, seg * n_hg + h))

    state = pl.BlockSpec((1, hp, HEAD_DIM, HEAD_DIM), lambda b, h, c: (b, h, 0, 0))
    in_specs = [col(4), col(5), col(6), col(7),
                pl.BlockSpec((chunk, LANES), lambda b, h, c: (b * n_c + c, 0)),
                cw(0), cw(1), cw(2), cb(0), cb(1), cb(2),
                pl.BlockSpec((2, LANES), lambda b, h, c: (0, 0)),
                pl.BlockSpec((1, HEAD_DIM), lambda b, h, c: (0, 0)), state]
    args = [proj, proj, proj, proj, proj_ab, conv_w, conv_w, conv_w, conv_buf, conv_buf, conv_buf,
            gate_params, norm_w.reshape(1, HEAD_DIM), s0]
    return pl.pallas_call(
        functools.partial(_gdn_kernel, chunk=chunk, precise=precise, hp=hp, n_h=n_h),
        grid=(bsz, n_hg, n_c), in_specs=in_specs,
        out_specs=(pl.BlockSpec((chunk, wide), lambda b, h, c: (b * n_c + c, h)), state),
        out_shape=(jax.ShapeDtypeStruct((bsz * seq, n_h * HEAD_DIM), F32 if precise else BF16),
                   jax.ShapeDtypeStruct(s0.shape, F32)),
        scratch_shapes=([pltpu.VMEM((HEAD_DIM, HEAD_DIM), F32)] * hp
                        + [pltpu.VMEM((3, SUBLANES + chunk, HEAD_DIM), F32)] * hp
                        + [pltpu.VMEM((chunk, HEAD_DIM), F32)] * hp),
        compiler_params=_params("parallel", "parallel", "arbitrary"), name="gdn")(*args)


def _sb_block(z, mask, run, u_incl):
    sp = _softplus(z)
    log_stay = jnp.where(mask, -sp, 0.0)
    incl = _dot01_right(log_stay, u_incl)
    later = incl - log_stay + run
    w = jnp.where(mask, jnp.exp((z - sp) + later), 0.0)
    return w, run + incl[:, 0:1]


def _sb_prompt_kernel(bias_ref, q_ref, k_ref, v_ref, o_ref, *scratch, hp):
    hg = pl.program_id(1)
    qi = pl.program_id(2)
    tq = q_ref.shape[0]
    acc_scrs, run_scrs = scratch[:hp], scratch[hp:]
    lanes = [slice(hh * HEAD_DIM, (hh + 1) * HEAD_DIM) for hh in range(hp)]
    qs = [q_ref[:, lanes[hh]].astype(BF16) for hh in range(hp)]
    biases = [bias_ref[hg * hp + hh] for hh in range(hp)]
    scale = HEAD_DIM ** -0.5
    for hh in range(hp):
        acc_scrs[hh][...] = jnp.zeros_like(acc_scrs[hh])
        run_scrs[hh][...] = jnp.zeros_like(run_scrs[hh])
    t_pos = qi * tq + lax.broadcasted_iota(jnp.int32, (tq, 1), 0)
    u_incl = _tri(SB_TK).astype(BF16)
    n_kb = (qi + 1) * (tq // SB_TK)

    def body(i, carry):
        k0 = pl.multiple_of((n_kb - 1 - i) * SB_TK, SB_TK)
        s_pos = k0 + lax.broadcasted_iota(jnp.int32, (1, SB_TK), 1)
        mask = s_pos < t_pos
        for hh in range(hp):
            kb = k_ref[pl.ds(k0, SB_TK), lanes[hh]].astype(BF16)
            vb = v_ref[pl.ds(k0, SB_TK), lanes[hh]].astype(BF16)
            z = _dot(qs[hh], kb, _NT) * scale + biases[hh]
            w, run = _sb_block(z, mask, run_scrs[hh][...], u_incl)
            acc_scrs[hh][...] += _dot(w.astype(BF16), vb)
            run_scrs[hh][...] = run
        return carry

    lax.fori_loop(0, n_kb, body, 0)
    for hh in range(hp):
        o_ref[:, lanes[hh]] = acc_scrs[hh][...].astype(o_ref.dtype)


def _sb_prompt(proj, bias, *, bsz, seq, n_h, hp, seg0):
    n_q = seq // SB_TQ
    n_hg = n_h // hp
    wide = hp * HEAD_DIM
    kv = lambda seg: pl.BlockSpec((seq, wide), lambda b, h, i, bias: (b, seg * n_hg + h))
    return pl.pallas_call(
        functools.partial(_sb_prompt_kernel, hp=hp),
        grid_spec=pltpu.PrefetchScalarGridSpec(
            num_scalar_prefetch=1, grid=(bsz, n_hg, n_q),
            in_specs=[pl.BlockSpec((SB_TQ, wide), lambda b, h, i, bias: (b * n_q + i, seg0 * n_hg + h)),
                      kv(seg0 + 1), kv(seg0 + 2)],
            out_specs=pl.BlockSpec((SB_TQ, wide), lambda b, h, i, bias: (b * n_q + i, h)),
            scratch_shapes=[pltpu.VMEM((SB_TQ, HEAD_DIM), F32)] * hp + [pltpu.VMEM((SB_TQ, 1), F32)] * hp),
        out_shape=jax.ShapeDtypeStruct((bsz * seq, n_h * HEAD_DIM), BF16),
        compiler_params=_params("parallel", "parallel", "arbitrary"), name="sb_prompt")(bias, proj, proj, proj)


def _sb_sample_kernel(pt_ref, bias_ref, q_ref, ko_ref, vo_ref, *rest, n_h, pages):
    kp_refs, vp_refs = rest[:pages], rest[pages:2 * pages]
    o_ref, acc_scr, run_scr, qall_scr, kown_scr, vown_scr = rest[2 * pages:]
    _sb_sample_body(bias_ref, q_ref, ko_ref, vo_ref, kp_refs, vp_refs, o_ref,
                    acc_scr, run_scr, qall_scr, kown_scr, vown_scr, n_h=n_h)
    del pt_ref


def _sb_sample_body(bias_ref, q_ref, ko_ref, vo_ref, kp_refs, vp_refs, o_ref,
                    acc_scr, run_scr, qall_scr, kown_scr, vown_scr, *, n_h):
    j = pl.program_id(1)
    n_j = pl.num_programs(1)
    t = q_ref.shape[0]
    n_pg = kp_refs[0].shape[0]
    t_bits = t.bit_length() - 1
    scale = HEAD_DIM ** -0.5
    lane = lax.broadcasted_iota(jnp.int32, (1, LANES), 1)
    lane_h = lax.shift_right_logical(lane, t_bits)
    lane_t = jnp.bitwise_and(lane, t - 1)
    bias_row = jnp.zeros((1, LANES), F32)
    for h in range(n_h):
        bias_row = jnp.where(lane_h == h, bias_ref[h], bias_row)

    def head_slice(h):
        return slice(h * HEAD_DIM, (h + 1) * HEAD_DIM)

    @pl.when(j == 0)
    def _():
        qall_scr[...] = jnp.zeros_like(qall_scr)
        kown_scr[...] = jnp.zeros_like(kown_scr)
        vown_scr[...] = jnp.zeros_like(vown_scr)
        for h in range(n_h):
            qall_scr[h * t:(h + 1) * t, :] = q_ref[:, head_slice(h)]
            kown_scr[h * t:(h + 1) * t, :] = ko_ref[:, head_slice(h)]
            vown_scr[h * t:(h + 1) * t, :] = vo_ref[:, head_slice(h)]
        z = _dot3(kown_scr[...], qall_scr[...], _NT) * scale + bias_row
        row = lax.broadcasted_iota(jnp.int32, (LANES, 1), 0)
        row_h = lax.shift_right_logical(row, t_bits)
        row_s = jnp.bitwise_and(row, t - 1)
        valid = jnp.logical_and(jnp.logical_and(row_h == lane_h, row_s < lane_t), lane < n_h * t)
        sp = _softplus(z)
        log_stay = jnp.where(valid, -sp, 0.0)
        later_keys =jnp.logical_and(row_h == lane_h, lane_t >= row_s)
        incl = _dot01_left(later_keys.astype(BF16), log_stay)
        w = jnp.where(valid, jnp.exp((z - sp) + (incl - log_stay)), 0.0)
        acc_scr[...] = _dot3(w.T, vown_scr[...])
        run_scr[...] = jnp.sum(log_stay, axis=0, keepdims=True)

    @pl.when(j > 0)
    def _():
        row_h = jnp.bitwise_and(lax.broadcasted_iota(jnp.int32, (n_pg * n_h, 1), 0), n_h - 1)
        valid = row_h == lane_h
        terms = []
        for kp_ref in kp_refs:
            k2 = kp_ref[...].reshape(n_pg * n_h, HEAD_DIM)
            z = _dot3(k2, qall_scr[...], _NT) * scale + bias_row
            sp = _softplus(z)
            log_stay = jnp.where(valid, -sp, 0.0)
            by_key = log_stay.reshape(n_pg, n_h, LANES)
            tail = jnp.zeros((n_h, LANES), F32)
            incl = [None] * n_pg
            for s in reversed(range(n_pg)):
                tail = tail + by_key[s]
                incl[s] = tail
            incl = jnp.stack(incl).reshape(n_pg * n_h, LANES)
            terms.append(((z - sp) + (incl - log_stay), jnp.sum(tail, axis=0, keepdims=True)))
        run = run_scr[...]
        acc = acc_scr[...]
        for (log_w, total), vp_ref in zip(terms, vp_refs):
            w = jnp.where(valid, jnp.exp(log_w + run), 0.0)
            acc = acc + _dot3(w.T, vp_ref[...].reshape(n_pg * n_h, HEAD_DIM))
            run = run + total
        acc_scr[...] = acc
        run_scr[...] = run

    @pl.when(j == n_j - 1)
    def _():
        for h in range(n_h):
            o_ref[:, head_slice(h)] = acc_scr[h * t:(h + 1) * t, :]


def _sb_sample(proj, cache_k, cache_v, layer, page_table, bias, *, bsz, seq, n_h):
    n_pages = page_table.shape[1]
    br = n_h * HEAD_DIM

    def own(seg):
        return pl.BlockSpec((seq, br), lambda b, j, pt, bias: (b, seg))

    pages = SB_PAGES_PER_STEP if n_pages % SB_PAGES_PER_STEP == 0 else 1

    def page(p):
        def index(b, j, pt, bias):
            return (layer, pt[b, n_pages - pages * (jnp.maximum(j, 1) - 1) - 1 - p], 0, 0, 0)
        return pl.BlockSpec((None, None, cache_k.shape[2], n_h, HEAD_DIM), index)

    page_specs = [page(p) for p in range(pages)]
    return pl.pallas_call(
        functools.partial(_sb_sample_kernel, n_h=n_h, pages=pages),
        grid_spec=pltpu.PrefetchScalarGridSpec(
            num_scalar_prefetch=2, grid=(bsz, n_pages // pages + 1),
            in_specs=[own(8), own(9), own(10)] + page_specs + page_specs,
            out_specs=pl.BlockSpec((seq, br), lambda b, j, pt, bias: (b, 0)),
            scratch_shapes=[pltpu.VMEM((LANES, HEAD_DIM), F32), pltpu.VMEM((1, LANES), F32)]
            + [pltpu.VMEM((LANES, HEAD_DIM), F32)] * 3),
        out_shape=jax.ShapeDtypeStruct((bsz * seq, br), F32),
        compiler_params=_params("parallel", "arbitrary"), name="sb_sample")(
            page_table, bias, proj, proj, proj, *([cache_k] * pages), *([cache_v] * pages))


def _router_kernel(x_ref, nw_ref, wr_ref, br_ref, h_ref, r_ref, *, precise):
    hn = _rms(x_ref[...], nw_ref[...])
    h_ref[...] = hn
    logits = _mm(precise)(hn, wr_ref[...]) + br_ref[...]
    lane = lax.broadcasted_iota(jnp.int32, logits.shape, 1)
    big = jnp.int32(LANES)

    def first_max(mask):
        m = jnp.max(jnp.where(mask, logits, NEG_INF), axis=-1, keepdims=True)
        idx = jnp.min(jnp.where(jnp.logical_and(mask, logits == m), lane, big), axis=-1, keepdims=True)
        return m, idx

    is_grp = lane < N_GROUPS
    g_max, grp = first_max(is_grp)
    p_grp = 1.0 / jnp.sum(jnp.where(is_grp, jnp.exp(logits - g_max), 0.0), axis=-1, keepdims=True)
    e_lo = N_GROUPS + grp * EXPERTS_PER_GROUP
    in_grp = jnp.logical_and(lane >= e_lo, lane < e_lo + EXPERTS_PER_GROUP)
    v1, i1 = first_max(in_grp)
    v2, i2 = first_max(jnp.logical_and(in_grp, lane != i1))
    e2 = jnp.exp(v2 - v1)
    w1 = (1.0 / (1.0 + e2)) * p_grp
    w2 = (e2 / (1.0 + e2)) * p_grp
    out = jnp.where(lane == 0, (i1 - N_GROUPS).astype(F32), 0.0)
    out = jnp.where(lane == 1, (i2 - N_GROUPS).astype(F32), out)
    out = jnp.where(lane == 2, w1, out)
    out = jnp.where(lane == 3, w2, out)
    r_ref[...] = out


def _router(x, norm_w, w_router, b_router, *, tm, precise):
    n, d = x.shape
    row = pl.BlockSpec((tm, d), lambda i: (i, 0))
    return pl.pallas_call(
        functools.partial(_router_kernel, precise=precise), grid=(n // tm,),
        in_specs=[row, pl.BlockSpec((1, d), lambda i: (0, 0)), pl.BlockSpec((d, LANES), lambda i: (0, 0)),
                  pl.BlockSpec((1, LANES), lambda i: (0, 0))],
        out_specs=(row, pl.BlockSpec((tm, LANES), lambda i: (i, 0))),
        out_shape=(jax.ShapeDtypeStruct((n, d), F32), jax.ShapeDtypeStruct((n, LANES), F32)),
        compiler_params=_params("parallel"), name="router")(x, norm_w.reshape(1, d), w_router, b_router)


def _new_expert(te_ref, i):
    return jnp.logical_or(i == 0, te_ref[i] != te_ref[jnp.maximum(i - 1, 0)])


def _expert_up_kernel(src_ref, te_ref, nt_ref, x_hbm, wg_ref, wu_ref, rw_ref, o_ref, xbuf, sems, wg_scr, wu_scr):
    i = pl.program_id(0)
    n_t = nt_ref[0]
    tm = o_ref.shape[0]

    def copies(step, start):
        slot = step % 2
        _row_copies(src_ref, step * tm, tm, x_hbm, xbuf.at[slot], sems.at[slot], start)

    @pl.when(i == 0)
    def _():
        copies(i, True)

    @pl.when(jnp.logical_and(i < n_t, _new_expert(te_ref, i)))
    def _():
        wg_scr[...] = wg_ref[...].astype(BF16)
        wu_scr[...] = wu_ref[...].astype(BF16)

    def compute(prefetch):
        copies(i, False)
        slot, nxt = i % 2, (i + 1) % 2
        d = xbuf.shape[2]
        kw = d // EXPERT_K_CHUNKS
        rows_per = tm // EXPERT_K_CHUNKS
        g = u = None
        for c in range(EXPERT_K_CHUNKS):
            xc = xbuf[slot, :, c * kw:(c + 1) * kw].astype(BF16)
            pg = _dot(xc, wg_scr[c * kw:(c + 1) * kw, :])
            pu = _dot(xc, wu_scr[c * kw:(c + 1) * kw, :])
            g = pg if g is None else g + pg
            u = pu if u is None else u + pu
            if prefetch:
                for r in range(c * rows_per, (c + 1) * rows_per):
                    pltpu.make_async_copy(x_hbm.at[pl.ds(src_ref[(i + 1) * tm + r], 1), :],
                                          xbuf.at[nxt].at[pl.ds(r, 1), :], sems.at[nxt]).start()
        o_ref[...] = ((_silu(g) * u) * rw_ref[...]).astype(o_ref.dtype)

    @pl.when(i + 1 < n_t)
    def _():
        compute(True)

    @pl.when(i + 1 == n_t)
    def _():
        compute(False)

    @pl.when(i >= n_t)
    def _():
        o_ref[...] = jnp.zeros_like(o_ref)


def _expert_down_kernel(te_ref, nt_ref, h_ref, wd_ref, o_ref, wd_scr):
    i = pl.program_id(0)

    @pl.when(jnp.logical_and(i < nt_ref[0], _new_expert(te_ref, i)))
    def _():
        wd_scr[...] = wd_ref[...].astype(BF16)

    @pl.when(i < nt_ref[0])
    def _():
        o_ref[...] = _dot(h_ref[...], wd_scr[...])

    @pl.when(i >= nt_ref[0])
    def _():
        o_ref[...] = jnp.zeros_like(o_ref)


def _experts(h2, src_tok, row_w, tile_expert, n_tiles, w_gate, w_up, w_down, layer):
    r = src_tok.shape[0]
    d = h2.shape[1]
    f = w_gate.shape[-1]
    n_t = r // MOE_TILE
    hid = pl.pallas_call(
        _expert_up_kernel,
        grid_spec=pltpu.PrefetchScalarGridSpec(
            num_scalar_prefetch=3, grid=(n_t,),
            in_specs=[pl.BlockSpec(memory_space=pl.ANY),
                      pl.BlockSpec((None, None, d, f), lambda i, src, te, nt: (layer, te[i], 0, 0)),
                      pl.BlockSpec((None, None, d, f), lambda i, src, te, nt: (layer, te[i], 0, 0)),
                      pl.BlockSpec((MOE_TILE, 1), lambda i, src, te, nt: (i, 0))],
            out_specs=pl.BlockSpec((MOE_TILE, f), lambda i, src, te, nt: (i, 0)),
            scratch_shapes=[pltpu.VMEM((2, MOE_TILE, d), F32), pltpu.SemaphoreType.DMA((2,)),
                            pltpu.VMEM((d, f), BF16), pltpu.VMEM((d, f), BF16)]),
        out_shape=jax.ShapeDtypeStruct((r, f), BF16),
        compiler_params=_params("arbitrary"), name="expert_up")(
            src_tok, tile_expert, n_tiles, h2, w_gate, w_up, row_w)
    return pl.pallas_call(
        _expert_down_kernel,
        grid_spec=pltpu.PrefetchScalarGridSpec(
            num_scalar_prefetch=2, grid=(n_t,),
            in_specs=[pl.BlockSpec((MOE_TILE, f), lambda i, te, nt: (i, 0)),
                      pl.BlockSpec((None, None, f, d), lambda i, te, nt: (layer, te[i], 0, 0))],
            out_specs=pl.BlockSpec((MOE_TILE, d), lambda i, te, nt: (i, 0)),
            scratch_shapes=[pltpu.VMEM((f, d), BF16)]),
        out_shape=jax.ShapeDtypeStruct((r, d), F32),
        compiler_params=_params("arbitrary"), name="expert_down")(tile_expert, n_tiles, hid, w_down)


def _moe_schedule(eid, wsel):
    n = eid.shape[0]
    flat_e = eid.reshape(-1)
    onehot = (flat_e[:, None] == jnp.arange(N_EXPERTS, dtype=jnp.int32)[None, :]).astype(jnp.int32)
    rank = jnp.sum((jnp.cumsum(onehot, axis=0) - onehot) * onehot, axis=1)
    counts = jnp.sum(onehot, axis=0)
    tiles_e = (counts + MOE_TILE - 1) // MOE_TILE
    tiles_end = jnp.cumsum(tiles_e)
    row_start = (tiles_end - tiles_e) * MOE_TILE
    dest = row_start[flat_e] + rank
    n_tiles_max = (2 * n + MOE_TILE - 1) // MOE_TILE + N_EXPERTS
    n_rows = n_tiles_max * MOE_TILE
    src_tok = jnp.zeros((n_rows,), jnp.int32).at[dest].set(jnp.arange(2 * n, dtype=jnp.int32) // 2)
    row_w = jnp.zeros((n_rows,), F32).at[dest].set(wsel.reshape(-1))
    n_tiles = tiles_end[-1]
    tile_ids = jnp.minimum(jnp.arange(n_tiles_max, dtype=jnp.int32), n_tiles - 1)
    tile_expert = jnp.sum((tile_ids[:, None] >= tiles_end[None, :]).astype(jnp.int32), axis=1)
    return dest, src_tok, row_w.reshape(n_rows, 1), tile_expert, n_tiles.reshape(1).astype(jnp.int32)


def _moe_sorted(x, norm_w, w_router, b_router, w_gate, w_up, w_down, layer):
    n = x.shape[0]
    h2, routed = _router(x, norm_w, w_router, b_router, tm=NORM_TILE, precise=False)
    eid = routed[:, 0:2].astype(jnp.int32)
    wsel = routed[:, 2:4]
    dest, src_tok, row_w, tile_expert, n_tiles = _moe_schedule(eid, wsel)
    y_rows = _experts(h2, src_tok, row_w, tile_expert, n_tiles, w_gate, w_up, w_down, layer)
    back = dest.reshape(n, 2).T.reshape(-1)
    return y_rows, back


def _experts_dense_kernel(x_ref, cw_ref, wg_ref, wu_ref, wd_ref, o_ref):
    e = pl.program_id(0)
    fi = pl.program_id(1)

    @pl.when(jnp.logical_and(e == 0, fi == 0))
    def _():
        o_ref[...] = jnp.zeros_like(o_ref)

    x = x_ref[...]
    lane = lax.broadcasted_iota(jnp.int32, (1, LANES), 1)
    cw = jnp.sum(jnp.where(lane == e, cw_ref[...], 0.0), axis=1, keepdims=True)
    hid = (_silu(_dot3(x, wg_ref[...])) * _dot3(x, wu_ref[...])) * cw
    o_ref[...] += _dot3(hid, wd_ref[...])


def _moe_dense(x, norm_w, w_router, b_router, w_gate, w_up, w_down, layer):
    n, d = x.shape
    f = w_gate.shape[-1]
    tf = f // 2
    h2, routed = _router(x, norm_w, w_router, b_router, tm=n, precise=True)
    eid = routed[:, 0:2].astype(jnp.int32)
    combine = jnp.sum(jax.nn.one_hot(eid, LANES, dtype=F32) * routed[:, 2:4, None], axis=1)
    full = lambda shape: pl.BlockSpec(shape, lambda e, fi: (0, 0))
    return pl.pallas_call(
        _experts_dense_kernel, grid=(N_EXPERTS, f // tf),
        in_specs=[full((n, d)), full((n, LANES)),
                  pl.BlockSpec((None, None, d, tf), lambda e, fi: (layer, e, 0, fi)),
                  pl.BlockSpec((None, None, d, tf), lambda e, fi: (layer, e, 0, fi)),
                  pl.BlockSpec((None, None, tf, d), lambda e, fi: (layer, e, fi, 0))],
        out_specs=full((n, d)), out_shape=jax.ShapeDtypeStruct((n, d), F32),
        compiler_params=_params("arbitrary", "arbitrary"), name="experts_dense")(
            h2, combine, w_gate, w_up, w_down)


def kernel(x_prompt, x_sample, state_hgrn, state_gdn, state_gdn_conv, cache_k, cache_v, page_table, norm_mix, w_in, hgrn_lb, hgrn_norm, gdn_conv, gdn_a_log, gdn_dt_bias, gdn_norm, sb_bias, w_branch, w_out, norm_ffn, w_router_group, b_router_group, w_router_expert, b_router_expert, w_exp_gate, w_exp_up, w_exp_down, final_norm):
    bp, tp, d = x_prompt.shape
    bs, ts, _ = x_sample.shape
    depth = w_in.shape[0]
    n_h = state_hgrn.shape[2]
    br = n_h * HEAD_DIM
    n_p, n_s = bp * tp, bs * ts
    assert n_p % IN_TILE == 0 and n_p % ROW_TILE == 0 and n_p % NORM_TILE == 0
    assert tp % CHUNK == 0 and tp % SB_TQ == 0 and ts % SUBLANES == 0 and n_s % SUBLANES == 0
    assert 2 * n_h <= LANES and br % PROJ_TN == 0
    assert n_h * ts <= LANES and ts & (ts - 1) == 0 and n_h & (n_h - 1) == 0

    heads_per_step = min(HEADS_PER_STEP, n_h)
    ab0 = 8 * br
    ab1 = ab0 + 2 * n_h
    n_main = 11 * br + 3 * d
    gate_col0 = 11 * br
    w_in_t = jnp.swapaxes(w_in, 1, 2)
    lb_all = jnp.cumsum(jax.nn.softmax(hgrn_lb.astype(F32), axis=0), axis=0)
    zeros_h = jnp.zeros((bp, n_h, HEAD_DIM, HEAD_DIM), F32)
    zeros_conv = jnp.zeros((bp, CONV_W - 1, 3 * br), F32)

    xp = x_prompt.reshape(n_p, d)
    xs = x_sample.reshape(n_s, d)
    outs_p = [[] for _ in range(5)]
    outs_s = [[] for _ in range(5)]
    hp = _rmsnorm(xp, norm_mix[0], BF16, ROW_TILE)
    hs = _rmsnorm(xs, norm_mix[0], F32, n_s)
    for l in range(depth):
        lb = (lb_all[l] - lb_all[0]).reshape(1, br)
        lb_params = (jnp.log(lb), jnp.log1p(-lb), 1.0 - lb)
        gate_params = jnp.pad(jnp.stack([gdn_a_log[l], gdn_dt_bias[l]]).astype(F32), ((0, 0), (0, LANES - n_h)))
        conv_w = gdn_conv[l].astype(F32)
        bias = sb_bias[l].astype(F32)
        w_router = jnp.pad(jnp.concatenate([w_router_group[l], w_router_expert[l]], axis=1),
                           ((0, 0), (0, LANES - N_GROUPS - N_EXPERTS)))
        b_router = jnp.pad(jnp.concatenate([b_router_group[l], b_router_expert[l]]).astype(F32),
                           (0, LANES - N_GROUPS - N_EXPERTS)).reshape(1, LANES)
        next_norm = norm_mix[l + 1] if l + 1 < depth else final_norm

        proj_ab_cols = _matmul_t(hp, w_in_t, l, tm=IN_TILE, tn=PROJ_TN, row0=0, n_cols=ab0, precise=False,
                                 name="in_proj_ab")
        proj_c_cols = _matmul_t(hp, w_in_t, l, tm=IN_TILE, tn=PROJ_TN, row0=ab1, n_cols=n_main - ab0,
                                precise=False, name="in_proj_c")
        proj_ab = _matmul_t(hp, w_in_t, l, tm=IN_TILE, tn=LANES, row0=ab0, n_cols=LANES, precise=False,
                            name="in_proj_gates")
        o_a, hg_p = _hgrn(proj_ab_cols, lb_params, hgrn_norm[l], zeros_h, seq=tp, chunk=CHUNK, precise=False,
                          hp=heads_per_step)
        o_b, gd_p = _gdn(proj_ab_cols, proj_ab, conv_w, zeros_conv, gate_params, gdn_norm[l], zeros_h,
                         seq=tp, chunk=CHUNK, precise=False, hp=heads_per_step)
        o_c = _sb_prompt(proj_c_cols, bias, bsz=bp, seq=tp, n_h=n_h, hp=min(SB_HEADS_PER_STEP, n_h), seg0=0)
        mixed = _merge(o_a, o_b, o_c, w_branch, l, proj_c_cols, gate_col0 - ab0, d, tm=ROW_TILE, precise=False)
        xp = _matmul(mixed, w_out, layer=l, tm=ROW_TILE, tn=PROJ_TN, residual=xp, weight_resident=True,
                     name="out_proj")
        y_rows, back = _moe_sorted(xp, norm_ffn[l], w_router, b_router, w_exp_gate, w_exp_up, w_exp_down, l)
        xp, hp = _moe_add_norm(xp, y_rows, back, next_norm, BF16 if l + 1 < depth else F32)
        outs_p[0].append(hg_p)
        outs_p[1].append(gd_p)
        outs_p[2].append(proj_ab_cols[:, 4 * br:7 * br].reshape(bp, tp, 3 * br)[:, tp - (CONV_W - 1):])
        outs_p[3].append(proj_c_cols[:, br:2 * br].reshape(bp, tp, n_h, HEAD_DIM))
        outs_p[4].append(proj_c_cols[:, 2 * br:3 * br].reshape(bp, tp, n_h, HEAD_DIM))

        proj_s = jnp.concatenate([
            _matmul_t(hs, w_in_t, l, tm=n_s, tn=PROJ_TN, row0=0, n_cols=ab0, precise=True, name="in_proj_s0"),
            _matmul_t(hs, w_in_t, l, tm=n_s, tn=PROJ_TN, row0=ab1, n_cols=n_main - ab0, precise=True,
                      name="in_proj_s1")], axis=1)
        proj_ab_s = _matmul_t(hs, w_in_t, l, tm=n_s, tn=LANES, row0=ab0, n_cols=LANES, precise=True,
                              name="in_proj_gates_s")
        oa_s, hg_s = _hgrn(proj_s, lb_params, hgrn_norm[l], state_hgrn[l].astype(F32), seq=ts, chunk=ts,
                           precise=True, hp=heads_per_step)
        ob_s, gd_s = _gdn(proj_s, proj_ab_s, conv_w, state_gdn_conv[l].astype(F32), gate_params, gdn_norm[l],
                          state_gdn[l].astype(F32), seq=ts, chunk=ts, precise=True, hp=heads_per_step)
        oc_s = _sb_sample(proj_s, cache_k, cache_v, l, page_table, bias, bsz=bs, seq=ts, n_h=n_h)
        mixed_s = _merge(oa_s, ob_s, oc_s, w_branch, l, proj_s, gate_col0, d, tm=n_s, precise=True)
        xs = _matmul_precise(mixed_s, w_out, layer=l, tn=PROJ_TN, residual=xs, name="out_proj_s")
        y_s = _moe_dense(xs, norm_ffn[l], w_router, b_router, w_exp_gate, w_exp_up, w_exp_down, l)
        xs, hs = _add_norm(xs, y_s, next_norm, F32)
        outs_s[0].append(hg_s)
        outs_s[1].append(gd_s)
        outs_s[2].append(proj_s[:, 4 * br:7 * br].reshape(bs, ts, 3 * br)[:, ts - (CONV_W - 1):])
        outs_s[3].append(proj_s[:, 9 * br:10 * br].reshape(bs, ts, n_h, HEAD_DIM))
        outs_s[4].append(proj_s[:, 10 * br:11 * br].reshape(bs, ts, n_h, HEAD_DIM))

    y_prompt = hp.reshape(bp, tp, d)
    y_sample = hs.reshape(bs, ts, d)
    return (y_prompt, y_sample, *(jnp.stack(o) for o in outs_p), *(jnp.stack(o) for o in outs_s))
```
